```python
import math, functools
import jax, jax.numpy as jnp
from jax import lax
import numpy as np

D_MODEL = 1024
BATCH = 16
SEQ = 4096
DEPTH = 1
DEC_BATCH = 128
DEC_SEQ = 8
PAST_LEN = 8192
PAGE_SIZE = 128

H_A = 4
DH_A = 64
DV_A = 2 * DH_A
H_R = 4
DK_R = 64
DV_R = 128
RET_CHUNK = 128
ROPE_BASE = 10000.0
QA_W = H_A * 2 * DH_A
KA_W = H_A * 2 * DH_A
VA_W = H_A * DV_A
QR_W = H_R * DK_R
KR_W = H_R * DK_R
VR_W = H_R * DV_R
GR_W = H_R * DV_R
D_IN = QA_W + KA_W + VA_W + QR_W + KR_W + VR_W + GR_W
D_MIX = VA_W + VR_W
N_MEM = 256
H_M = 4
DH_M = D_MODEL // H_M
D_FF = 2816
CONV_W = 3
Q_BLOCK = 128
EPS = 1e-6
NEG = -1e30

kernel_name = "hymba_diffattn_retnet_convffn_step"


def _rmsnorm(x, g):
    xf = x.astype(jnp.float32)
    y = xf * lax.rsqrt(jnp.mean(xf * xf, axis=-1, keepdims=True) + EPS)
    return (y * g.astype(jnp.float32)).astype(x.dtype)


def _rmsnorm_plain(x):
    xf = x.astype(jnp.float32)
    return xf * lax.rsqrt(jnp.mean(xf * xf, axis=-1, keepdims=True) + EPS)


def _rotary(x, pos):
    half = x.shape[-1] // 2
    inv = 1.0 / (ROPE_BASE ** jnp.linspace(0.0, 1.0, half, dtype=jnp.float32))
    ang = pos.astype(jnp.float32)[:, None] * inv[None, :]
    cos = jnp.cos(ang)[:, None, :]
    sin = jnp.sin(ang)[:, None, :]
    xf = x.astype(jnp.float32)
    x1, x2 = xf[..., :half], xf[..., half:]
    return jnp.concatenate([x1 * cos - x2 * sin, x1 * sin + x2 * cos], axis=-1)


def _split_proj(xn, w_in):
    B, T, _ = xn.shape
    u = xn @ w_in
    o1 = QA_W
    o2 = o1 + KA_W
    o3 = o2 + VA_W
    o4 = o3 + QR_W
    o5 = o4 + KR_W
    o6 = o5 + VR_W
    qa = u[..., :o1].reshape(B, T, H_A, 2, DH_A)
    ka = u[..., o1:o2].reshape(B, T, H_A, 2, DH_A)
    va = u[..., o2:o3].reshape(B, T, H_A, DV_A)
    qr = u[..., o3:o4].reshape(B, T, H_R, DK_R)
    kr = u[..., o4:o5].reshape(B, T, H_R, DK_R)
    vr = u[..., o5:o6].reshape(B, T, H_R, DV_R)
    gr = u[..., o6:]
    return qa, ka, va, qr, kr, vr, gr


def _diff_lambda(lam_q1, lam_k1, lam_q2, lam_k2, lam_init):
    f = jnp.float32
    return (jnp.exp(jnp.sum(lam_q1.astype(f) * lam_k1.astype(f)))
            - jnp.exp(jnp.sum(lam_q2.astype(f) * lam_k2.astype(f))) + lam_init)


def _diff_attn_prompt(qa, ka, va, lam):
    B, S = qa.shape[:2]
    nb = S // Q_BLOCK
    qb = jnp.moveaxis(qa.reshape(B, nb, Q_BLOCK, H_A, 2, DH_A), 1, 0)
    kpos = jnp.arange(S)
    scale = DH_A ** -0.5

    def block(args):
        qi, i = args
        s = jnp.einsum('bqhcd,bkhcd->bhcqk', qi, ka).astype(jnp.float32) * scale
        qpos = i * Q_BLOCK + jnp.arange(Q_BLOCK)
        mask = kpos[None, :] <= qpos[:, None]
        p = jax.nn.softmax(jnp.where(mask, s, NEG), axis=-1)
        a = p[:, :, 0] - lam * p[:, :, 1]
        return jnp.einsum('bhqk,bkhe->bqhe', a.astype(va.dtype), va)

    out = lax.map(block, (qb, jnp.arange(nb)))
    return jnp.moveaxis(out, 0, 1).reshape(B, S, H_A, DV_A)


def _diff_attn_sample(qa, ka, va, lam, cache_k, cache_v, page_table):
    T = qa.shape[1]
    past = page_table.shape[1] * PAGE_SIZE
    kidx = jnp.arange(past + T)
    qidx = jnp.arange(T)
    mask = (kidx[None, :] < past) | ((kidx[None, :] - past) <= qidx[:, None])
    scale = DH_A ** -0.5

    def one(args):
        q, kn, vn, pt = args
        kp = cache_k[pt].reshape(past, H_A, 2, DH_A)
        vp = cache_v[pt].reshape(past, H_A, DV_A)
        k = jnp.concatenate([kp, kn.astype(kp.dtype)], axis=0)
        v = jnp.concatenate([vp, vn.astype(vp.dtype)], axis=0)
        s = jnp.einsum('qhcd,khcd->hcqk', q, k).astype(jnp.float32) * scale
        p = jax.nn.softmax(jnp.where(mask, s, NEG), axis=-1)
        a = p[:, 0] - lam * p[:, 1]
        return jnp.einsum('hqk,khe->qhe', a.astype(v.dtype), v).astype(q.dtype)

    return lax.map(one, (qa, ka, va, page_table))


def _ret_chunk(S, q, k, v, log_g):
    C = q.shape[1]
    i = jnp.arange(C, dtype=jnp.float32)
    diff = i[:, None] - i[None, :]
    D = jnp.where(diff[None] >= 0, jnp.exp(jnp.maximum(diff, 0.0)[None] * log_g[:, None, None]), 0.0)
    s = jnp.einsum('bihd,bjhd->bhij', q, k) * D[None]
    intra = jnp.einsum('bhij,bjhe->bihe', s, v)
    qdec = jnp.exp((i + 1.0)[:, None] * log_g[None, :])[None, :, :, None]
    cross = jnp.einsum('bihd,bhde->bihe', q, S) * qdec
    kdec = jnp.exp((C - 1.0 - i)[:, None] * log_g[None, :])[None, :, :, None]
    S_new = S * jnp.exp(C * log_g)[None, :, None, None] + jnp.einsum('bjhd,bjhe->bhde', k * kdec, v)
    return S_new, intra + cross


def _retention(q, k, v, S0):
    B, T = q.shape[:2]
    C = RET_CHUNK if T % RET_CHUNK == 0 else T
    nc = T // C
    log_g = jnp.log(1.0 - 2.0 ** (-5.0 - jnp.arange(H_R, dtype=jnp.float32)))
    chunks = tuple(jnp.moveaxis(a.astype(jnp.float32).reshape(B, nc, C, *a.shape[2:]), 1, 0) for a in (q, k, v))
    S, o = lax.scan(lambda s, c: _ret_chunk(s, c[0], c[1], c[2], log_g), S0.astype(jnp.float32), chunks)
    return jnp.moveaxis(o, 0, 1).reshape(B, T, H_R, DV_R), S


def _mem_kv(mem, g_mem, w_mk, w_mv):
    B = mem.shape[0]
    mn = _rmsnorm(mem, g_mem)
    return (mn @ w_mk).reshape(B, N_MEM, H_M, DH_M), (mn @ w_mv).reshape(B, N_MEM, H_M, DH_M)


def _mem_attn(xn, mk, mv, w_mq, w_mo):
    B, T, _ = xn.shape
    q = (xn @ w_mq).reshape(B, T, H_M, DH_M)
    s = jnp.einsum('bqhd,bkhd->bhqk', q, mk.astype(q.dtype)).astype(jnp.float32) * DH_M ** -0.5
    p = jax.nn.softmax(s, axis=-1).astype(q.dtype)
    o = jnp.einsum('bhqk,bkhd->bqhd', p, mv.astype(q.dtype)).reshape(B, T, H_M * DH_M)
    return o @ w_mo


def _conv_ffn(xn, prev, w_gate, w_up, conv_w, conv_b, w_down):
    T = xn.shape[1]
    a = xn @ w_gate
    ap = jnp.concatenate([prev.astype(a.dtype), a], axis=1)
    c = conv_b
    for j in range(CONV_W):
        c = c + ap[:, j:j + T] * conv_w[j]
    h = jax.nn.gelu(c, approximate=False) * (xn @ w_up)
    return h @ w_down, ap[:, ap.shape[1] - (CONV_W - 1):]


def _block(h, pos, mk, mv, ret_state0, conv_prev, diff_fn, lam_init,
           w_in, w_out, lam_q1, lam_k1, lam_q2, lam_k2, g_diff_head,
           g_pre_mix, g_post_mix, g_pre_mem, g_post_mem, w_mq, w_mo,
           g_pre_ffn, g_post_ffn, w_gate, w_up, conv_w, conv_b, w_down):
    B, T, _ = h.shape
    xn = _rmsnorm(h, g_pre_mix)
    qa, ka, va, qr, kr, vr, gr = _split_proj(xn, w_in)
    lam = _diff_lambda(lam_q1, lam_k1, lam_q2, lam_k2, lam_init)
    oa = diff_fn(qa, ka, va, lam)
    oa = (_rmsnorm(oa, g_diff_head) * (1.0 - lam_init)).reshape(B, T, VA_W)
    qr_r = _rotary(qr, pos)
    kr_r = _rotary(kr, pos) * DK_R ** -0.5
    orr, S_new = _retention(qr_r, kr_r, vr, ret_state0)
    orr = _rmsnorm_plain(orr).reshape(B, T, VR_W) * jax.nn.silu(gr.astype(jnp.float32))
    mix = jnp.concatenate([oa.astype(h.dtype), orr.astype(h.dtype)], axis=-1) @ w_out
    h = h + _rmsnorm(mix, g_post_mix)
    m = _mem_attn(_rmsnorm(h, g_pre_mem), mk, mv, w_mq, w_mo)
    h = h + _rmsnorm(m, g_post_mem)
    f, conv_new = _conv_ffn(_rmsnorm(h, g_pre_ffn), conv_prev, w_gate, w_up, conv_w, conv_b, w_down)
    h = h + _rmsnorm(f, g_post_ffn)
    return h, ka.reshape(B, T, H_A, 2 * DH_A), va, S_new, conv_new


def setup_inputs(seed: int = 0) -> dict:
    key = jax.random.key(seed)
    keys = iter(jax.random.split(key, 48))
    f32 = jnp.float32
    n_pages = PAST_LEN // PAGE_SIZE
    n_used = DEC_BATCH * n_pages
    n_pool = (5 * n_used + 3) // 4

    def nrm(shape, scale):
        return jax.random.normal(next(keys), shape, f32) * scale

    def gain(shape):
        return 1.0 + 0.02 * jax.random.normal(next(keys), shape, f32)

    page_table = jax.random.permutation(next(keys), n_pool)[:n_used].reshape(DEC_BATCH, n_pages).astype(jnp.int32)
    return {
        "x_prompt": nrm((BATCH, SEQ, D_MODEL), 1.0),
        "x_sample": nrm((DEC_BATCH, DEC_SEQ, D_MODEL), 1.0),
        "cache_k": nrm((DEPTH, n_pool, PAGE_SIZE, H_A, 2 * DH_A), 1.0),
        "cache_v": nrm((DEPTH, n_pool, PAGE_SIZE, H_A, DV_A), 1.0),
        "cache_mem_k": nrm((DEPTH, DEC_BATCH, N_MEM, H_M, DH_M), 1.0),
        "cache_mem_v": nrm((DEPTH, DEC_BATCH, N_MEM, H_M, DH_M), 1.0),
        "state_ret": nrm((DEPTH, DEC_BATCH, H_R, DK_R, DV_R), 1.0),
        "state_conv": nrm((DEPTH, DEC_BATCH, CONV_W - 1, D_FF), 1.0),
        "page_table": page_table,
        "mem_prompt": nrm((BATCH, N_MEM, D_MODEL), 1.0),
        "w_in": nrm((DEPTH, D_MODEL, D_IN), D_MODEL ** -0.5),
        "w_out": nrm((DEPTH, D_MIX, D_MODEL), D_MIX ** -0.5),
        "lam_q1": nrm((DEPTH, DH_A), 0.1),
        "lam_k1": nrm((DEPTH, DH_A), 0.1),
        "lam_q2": nrm((DEPTH, DH_A), 0.1),
        "lam_k2": nrm((DEPTH, DH_A), 0.1),
        "g_diff_head": gain((DEPTH, DV_A)),
        "g_pre_mix": gain((DEPTH, D_MODEL)),
        "g_post_mix": gain((DEPTH, D_MODEL)),
        "g_pre_mem": gain((DEPTH, D_MODEL)),
        "g_post_mem": gain((DEPTH, D_MODEL)),
        "g_mem_in": gain((DEPTH, D_MODEL)),
        "w_mq": nrm((DEPTH, D_MODEL, D_MODEL), D_MODEL ** -0.5),
        "w_mk": nrm((DEPTH, D_MODEL, D_MODEL), D_MODEL ** -0.5),
        "w_mv": nrm((DEPTH, D_MODEL, D_MODEL), D_MODEL ** -0.5),
        "w_mo": nrm((DEPTH, D_MODEL, D_MODEL), D_MODEL ** -0.5),
        "g_pre_ffn": gain((DEPTH, D_MODEL)),
        "g_post_ffn": gain((DEPTH, D_MODEL)),
        "w_gate": nrm((DEPTH, D_MODEL, D_FF), D_MODEL ** -0.5),
        "w_up": nrm((DEPTH, D_MODEL, D_FF), D_MODEL ** -0.5),
        "conv_w": nrm((DEPTH, CONV_W, D_FF), CONV_W ** -0.5),
        "conv_b": nrm((DEPTH, D_FF), 0.02),
        "w_down": nrm((DEPTH, D_FF, D_MODEL), D_FF ** -0.5),
    }


def reference(x_prompt, x_sample, cache_k, cache_v, cache_mem_k, cache_mem_v, state_ret, state_conv,
              page_table, mem_prompt, w_in, w_out, lam_q1, lam_k1, lam_q2, lam_k2, g_diff_head,
              g_pre_mix, g_post_mix, g_pre_mem, g_post_mem, g_mem_in, w_mq, w_mk, w_mv, w_mo,
              g_pre_ffn, g_post_ffn, w_gate, w_up, conv_w, conv_b, w_down):
    B, S, _ = x_prompt.shape
    Bd, T, _ = x_sample.shape
    pos_p = jnp.arange(S)
    pos_s = PAST_LEN + jnp.arange(T)
    hp, hs = x_prompt, x_sample
    kp_l, vp_l, mkp_l, mvp_l, rp_l, cp_l = [], [], [], [], [], []
    ks_l, vs_l, rs_l, cs_l = [], [], [], []
    for l in range(DEPTH):
        lam_init = 0.8 - 0.6 * math.exp(-0.3 * l)
        wts = (w_in[l], w_out[l], lam_q1[l], lam_k1[l], lam_q2[l], lam_k2[l], g_diff_head[l],
               g_pre_mix[l], g_post_mix[l], g_pre_mem[l], g_post_mem[l], w_mq[l], w_mo[l],
               g_pre_ffn[l], g_post_ffn[l], w_gate[l], w_up[l], conv_w[l], conv_b[l], w_down[l])
        mk_p, mv_p = _mem_kv(mem_prompt, g_mem_in[l], w_mk[l], w_mv[l])
        ret0 = jnp.zeros((B, H_R, DK_R, DV_R), jnp.float32)
        conv0 = jnp.zeros((B, CONV_W - 1, D_FF), hp.dtype)
        hp, k_p, v_p, S_p, c_p = _block(hp, pos_p, mk_p, mv_p, ret0, conv0, _diff_attn_prompt, lam_init, *wts)
        diff_s = functools.partial(_diff_attn_sample, cache_k=cache_k[l], cache_v=cache_v[l], page_table=page_table)
        hs, k_s, v_s, S_s, c_s = _block(hs, pos_s, cache_mem_k[l], cache_mem_v[l], state_ret[l], state_conv[l],
                                        diff_s, lam_init, *wts)
        kp_l.append(k_p)
        vp_l.append(v_p)
        mkp_l.append(mk_p)
        mvp_l.append(mv_p)
        rp_l.append(S_p.astype(x_prompt.dtype))
        cp_l.append(c_p)
        ks_l.append(k_s.astype(cache_k.dtype))
        vs_l.append(v_s.astype(cache_v.dtype))
        rs_l.append(S_s.astype(state_ret.dtype))
        cs_l.append(c_s.astype(state_conv.dtype))
    return (hp, hs, jnp.stack(kp_l), jnp.stack(vp_l), jnp.stack(mkp_l), jnp.stack(mvp_l), jnp.stack(rp_l),
            jnp.stack(cp_l), jnp.stack(ks_l), jnp.stack(vs_l), jnp.stack(rs_l), jnp.stack(cs_l))
```

```python
import functools
import math

import jax
import jax.numpy as jnp
from jax import lax
from jax.experimental import pallas as pl
from jax.experimental.pallas import tpu as pltpu

F32 = jnp.float32
BF16 = jnp.bfloat16

H_A = 4
DH_A = 64
DV_A = 2 * DH_A
H_R = 4
DK_R = 64
DV_R = 128
RET_CHUNK = 128
ROPE_BASE = 10000.0
H_M = 4
CONV_W = 3
PAGE_SIZE = 128
EPS = 1e-6
NEG = -1e30

QA_W = H_A * 2 * DH_A
KA_W = H_A * 2 * DH_A
VA_W = H_A * DV_A
QR_W = H_R * DK_R
KR_W = H_R * DK_R
VR_W = H_R * DV_R
GR_W = H_R * DV_R

VMEM_LIMIT = 56 * 1024 * 1024
SUBLANES = 8

_NT = (((1,), (1,)), ((), ()))
_TN = (((0,), (0,)), ((), ()))


def _params(*sem):
    return pltpu.CompilerParams(dimension_semantics=sem, vmem_limit_bytes=VMEM_LIMIT)


def _const_spec(shape):
    nd = len(shape)
    return pl.BlockSpec(shape, lambda *_: (0,) * nd, pipeline_mode=pl.Buffered(1))


def _rms(x, g=None):
    y = x * lax.rsqrt(jnp.mean(x * x, axis=-1, keepdims=True) + EPS)
    return y if g is None else y * g


def _gelu_exact(x):
    return 0.5 * x * (1.0 + lax.erf(x * (2.0 ** -0.5)))


def _diff_lambda(lq1, lk1, lq2, lk2, lam_init):
    a = jnp.sum(lq1[...] * lk1[...], axis=-1, keepdims=True)
    b = jnp.sum(lq2[...] * lk2[...], axis=-1, keepdims=True)
    return jnp.exp(a) - jnp.exp(b) + lam_init


def _rotate(x, cos_t, sin_t):
    half = DK_R // 2
    w = x.shape[-1]
    lane = lax.broadcasted_iota(jnp.int32, x.shape, 1)
    swapped = jnp.where(lane % DK_R < half, pltpu.roll(x, w - half, 1), pltpu.roll(x, half, 1))
    return x * cos_t + swapped * sin_t


def _inproj_kernel(x_ref, g_ref, w_ref, cq_ref, sq_ref, ck_ref, sk_ref, *out_refs, with_vt, tkv):
    if with_vt:
        qa_ref, ka_ref, va_ref, kab_ref, vat_ref, qr_ref, kr_ref, vr_ref, gr_ref = out_refs
    else:
        qa_ref, ka_ref, va_ref, qr_ref, kr_ref, vr_ref, gr_ref = out_refs
    xn = _rms(x_ref[...], g_ref[...]).astype(BF16)

    def proj(c0, width):
        return jnp.dot(xn, w_ref[:, c0:c0 + width], preferred_element_type=F32)

    c = 0
    qa_ref[...] = (proj(c, QA_W) * DH_A ** -0.5).astype(qa_ref.dtype)
    c += QA_W
    ka = proj(c, KA_W)
    ka_ref[...] = ka
    c += KA_W
    va = proj(c, VA_W)
    va_ref[...] = va
    c += VA_W
    if with_vt:
        kab_ref[...] = ka.astype(BF16)
        for r in range(va.shape[0] // tkv):
            vat_ref[r] = va[r * tkv:(r + 1) * tkv, :].T.astype(BF16)
    qr_ref[...] = _rotate(proj(c, QR_W), cq_ref[...], sq_ref[...]).astype(qr_ref.dtype)
    c += QR_W
    kr_ref[...] = _rotate(proj(c, KR_W), ck_ref[...], sk_ref[...]).astype(kr_ref.dtype)
    c += KR_W
    vr_ref[...] = proj(c, VR_W).astype(vr_ref.dtype)
    c += VR_W
    gr_ref[...] = proj(c, GR_W)


def _inproj(x2d, g, w_in, tabs, *, tm, with_vt, tkv, act_dtype):
    n, d = x2d.shape
    period = tabs[0].shape[0]
    npb = period // tm
    row = lambda i: (i, 0)
    tab = lambda i: (i % npb, 0)
    in_specs = [pl.BlockSpec((tm, d), row), _const_spec((1, d)), _const_spec(w_in.shape)]
    in_specs += [pl.BlockSpec((tm, QR_W), tab)] * 4
    out_shape = [jax.ShapeDtypeStruct((n, QA_W), act_dtype),
                 jax.ShapeDtypeStruct((n, KA_W), F32),
                 jax.ShapeDtypeStruct((n, VA_W), F32)]
    out_specs = [pl.BlockSpec((tm, QA_W), row), pl.BlockSpec((tm, KA_W), row), pl.BlockSpec((tm, VA_W), row)]
    if with_vt:
        out_shape += [jax.ShapeDtypeStruct((n, KA_W), BF16),
                      jax.ShapeDtypeStruct((n // tkv, VA_W, tkv), BF16)]
        out_specs += [pl.BlockSpec((tm, KA_W), row),
                      pl.BlockSpec((tm // tkv, VA_W, tkv), lambda i: (i, 0, 0))]
    out_shape += [jax.ShapeDtypeStruct((n, QR_W), act_dtype),
                  jax.ShapeDtypeStruct((n, KR_W), act_dtype),
                  jax.ShapeDtypeStruct((n, VR_W), act_dtype),
                  jax.ShapeDtypeStruct((n, GR_W), F32)]
    out_specs += [pl.BlockSpec((tm, QR_W), row), pl.BlockSpec((tm, KR_W), row),
                  pl.BlockSpec((tm, VR_W), row), pl.BlockSpec((tm, GR_W), row)]
    return pl.pallas_call(
        functools.partial(_inproj_kernel, with_vt=with_vt, tkv=tkv),
        grid=(n // tm,), in_specs=in_specs, out_specs=out_specs, out_shape=out_shape,
        compiler_params=_params("parallel"), name="inproj",
    )(x2d, g, w_in, *tabs)


def _dattn_p_kernel(lq1, lk1, lq2, lk2, g_ref, q_ref, k_ref, vt_ref, o_ref, *, tq, tk, lam_init):
    qi = pl.program_id(2)
    lam = _diff_lambda(lq1, lk1, lq2, lk2, lam_init)
    q = q_ref[...]
    lane = lax.broadcasted_iota(jnp.int32, q.shape, 1)
    zero = jnp.zeros_like(q)
    qbd = jnp.concatenate([jnp.where(lane < DH_A, q, zero), jnp.where(lane >= DH_A, q, zero)], axis=0)

    def step(j, carry, masked):
        m, l, acc = carry
        kj = k_ref[pl.ds(pl.multiple_of(j * tk, tk), tk), :]
        st = lax.dot_general(kj, qbd, _NT, preferred_element_type=F32)
        if masked:
            kpos = j * tk + lax.broadcasted_iota(jnp.int32, st.shape, 0)
            qpos = qi * tq + lax.broadcasted_iota(jnp.int32, st.shape, 1) % tq
            st = jnp.where(kpos <= qpos, st, NEG)
        m_new = jnp.maximum(m, jnp.max(st, axis=0, keepdims=True))
        alpha = jnp.exp(m - m_new)
        p = jnp.exp(st - m_new)
        l = alpha * l + jnp.sum(p, axis=0, keepdims=True)
        acc = acc * alpha + jnp.dot(vt_ref[j], p.astype(BF16), preferred_element_type=F32)
        return m_new, l, acc

    carry = (jnp.full((1, 2 * tq), NEG, F32), jnp.zeros((1, 2 * tq), F32), jnp.zeros((DV_A, 2 * tq), F32))
    nfull = qi * (tq // tk)
    carry = lax.fori_loop(0, nfull, lambda j, c: step(j, c, False), carry)
    for d in range(tq // tk):
        carry = step(nfull + d, carry, True)
    _, l, acc = carry
    o = acc * (1.0 / l)
    o = o[:, :tq] - lam * o[:, tq:]
    y = o * lax.rsqrt(jnp.mean(o * o, axis=0, keepdims=True) + EPS) * g_ref[...] * (1.0 - lam_init)
    o_ref[...] = y.T.astype(o_ref.dtype)


def _dattn_prompt(lams, g_col, qa, kab, vat, *, b, s, tq, tk, lam_init):
    n = qa.shape[0]
    hw = 2 * DH_A
    nq = s // tq
    lam_specs = [_const_spec((1, DH_A))] * 4
    return pl.pallas_call(
        functools.partial(_dattn_p_kernel, tq=tq, tk=tk, lam_init=lam_init),
        grid=(b, H_A, nq),
        in_specs=lam_specs + [
            _const_spec((DV_A, 1)),
            pl.BlockSpec((tq, hw), lambda bi, h, qi: (bi * nq + qi, h)),
            pl.BlockSpec((s, hw), lambda bi, h, qi: (bi, h)),
            pl.BlockSpec((s // tk, DV_A, tk), lambda bi, h, qi: (bi, h, 0)),
        ],
        out_specs=pl.BlockSpec((tq, DV_A), lambda bi, h, qi: (bi * nq + qi, h)),
        out_shape=jax.ShapeDtypeStruct((n, VA_W), BF16),
        compiler_params=_params("parallel", "parallel", "arbitrary"), name="dattn_prompt",
    )(*lams, g_col, qa, kab, vat)


def _dattn_s_kernel(pt_ref, lq1, lk1, lq2, lk2, g_ref, q_ref, kn_ref, vn_ref, ck_hbm, cv_hbm, o_ref,
                    kbuf, vbuf, sems, m_scr, l_scr, acc_scr, *, pp, t, lam_init):
    bi = pl.program_id(0)
    gi = pl.program_id(1)
    ng = pl.num_programs(1)
    nsteps = pl.num_programs(0) * ng
    step = bi * ng + gi
    slot = step % 2
    rows = PAGE_SIZE * H_A
    hm = 2 * t
    nkeys = pp * PAGE_SIZE

    def copies(step_, slot_):
        b_ = step_ // ng
        g_ = step_ % ng
        out = []
        for p in range(pp):
            page = pt_ref[b_, g_ * pp + p]
            out.append(pltpu.make_async_copy(ck_hbm.at[page], kbuf.at[slot_, pl.ds(p * rows, rows)], sems.at[0, slot_]))
            out.append(pltpu.make_async_copy(cv_hbm.at[page], vbuf.at[slot_, pl.ds(p * rows, rows)], sems.at[1, slot_]))
        return out

    @pl.when(step == 0)
    def _():
        for c in copies(step, slot):
            c.start()

    @pl.when(step + 1 < nsteps)
    def _():
        for c in copies(step + 1, 1 - slot):
            c.start()

    @pl.when(gi == 0)
    def _():
        m_scr[...] = jnp.full(m_scr.shape, NEG, F32)
        l_scr[...] = jnp.zeros(l_scr.shape, F32)
        acc_scr[...] = jnp.zeros(acc_scr.shape, F32)

    q = q_ref[0]
    qbd = []
    for h in range(H_A):
        qh = q[:, h * 2 * DH_A:(h + 1) * 2 * DH_A]
        lane = lax.broadcasted_iota(jnp.int32, qh.shape, 1)
        qbd.append(jnp.concatenate([jnp.where(lane < DH_A, qh, 0.0), jnp.where(lane >= DH_A, qh, 0.0)], axis=0))

    def update(s_all, v_heads):
        m = m_scr[...]
        m_new = jnp.maximum(m, jnp.max(s_all, axis=-1, keepdims=True))
        alpha = jnp.exp(m - m_new)
        p = jnp.exp(s_all - m_new)
        l_scr[...] = alpha * l_scr[...] + jnp.sum(p, axis=-1, keepdims=True)
        m_scr[...] = m_new
        pv = [jnp.dot(p[h * hm:(h + 1) * hm].astype(v_heads[h].dtype), v_heads[h], preferred_element_type=F32)
              for h in range(H_A)]
        acc_scr[...] = acc_scr[...] * alpha + jnp.concatenate(pv, axis=0)

    for c in copies(step, slot):
        c.wait()

    s_heads, v_heads = [], []
    for h in range(H_A):
        kh = kbuf[slot, pl.ds(h, nkeys, stride=H_A), :].astype(BF16)
        v_heads.append(vbuf[slot, pl.ds(h, nkeys, stride=H_A), :].astype(BF16))
        s_heads.append(lax.dot_general(qbd[h].astype(BF16), kh, _NT, preferred_element_type=F32))
    update(jnp.concatenate(s_heads, axis=0), v_heads)

    @pl.when(gi == ng - 1)
    def _():
        kn = kn_ref[0]
        vn = vn_ref[0]
        s_heads, v_heads = [], []
        for h in range(H_A):
            s_heads.append(lax.dot_general(qbd[h], kn[:, h * 2 * DH_A:(h + 1) * 2 * DH_A], _NT,
                                           preferred_element_type=F32))
            v_heads.append(vn[:, h * DV_A:(h + 1) * DV_A])
        s_new = jnp.concatenate(s_heads, axis=0)
        kidx = lax.broadcasted_iota(jnp.int32, s_new.shape, 1)
        qidx = lax.broadcasted_iota(jnp.int32, s_new.shape, 0) % t
        update(jnp.where(kidx <= qidx, s_new, NEG), v_heads)

        lam = _diff_lambda(lq1, lk1, lq2, lk2, lam_init)
        o = acc_scr[...] * (1.0 / l_scr[...])
        for h in range(H_A):
            oh = o[h * hm:h * hm + t] - lam * o[h * hm + t:(h + 1) * hm]
            y = _rms(oh, g_ref[...]) * (1.0 - lam_init)
            o_ref[0, :, h * DV_A:(h + 1) * DV_A] = y.astype(o_ref.dtype)


def _dattn_sample(page_table, lams, g_row, qa, kn, vn, ck, cv, *, pp, lam_init):
    bd, t, _ = qa.shape
    npages = page_table.shape[1]
    rows = PAGE_SIZE * H_A
    blk = lambda bi, gi, pt: (bi, 0, 0)
    cst = lambda bi, gi, pt: (0, 0)
    grid_spec = pltpu.PrefetchScalarGridSpec(
        num_scalar_prefetch=1,
        grid=(bd, npages // pp),
        in_specs=[pl.BlockSpec((1, DH_A), cst)] * 4 + [
            pl.BlockSpec((1, DV_A), cst),
            pl.BlockSpec((1, t, QA_W), blk),
            pl.BlockSpec((1, t, KA_W), blk),
            pl.BlockSpec((1, t, VA_W), blk),
            pl.BlockSpec(memory_space=pl.ANY),
            pl.BlockSpec(memory_space=pl.ANY),
        ],
        out_specs=pl.BlockSpec((1, t, VA_W), blk),
        scratch_shapes=[
            pltpu.VMEM((2, pp * rows, 2 * DH_A), F32),
            pltpu.VMEM((2, pp * rows, DV_A), F32),
            pltpu.SemaphoreType.DMA((2, 2)),
            pltpu.VMEM((H_A * 2 * t, 1), F32),
            pltpu.VMEM((H_A * 2 * t, 1), F32),
            pltpu.VMEM((H_A * 2 * t, DV_A), F32),
        ],
    )
    return pl.pallas_call(
        functools.partial(_dattn_s_kernel, pp=pp, t=t, lam_init=lam_init),
        grid_spec=grid_spec,
        out_shape=jax.ShapeDtypeStruct((bd, t, VA_W), F32),
        compiler_params=_params("arbitrary", "arbitrary"), name="dattn_sample",
    )(page_table, *lams, g_row, qa, kn, vn, ck, cv)


def _ret_kernel(q_ref, k_ref, v_ref, g_ref, s0_ref, dmat_ref, qdec_ref, kdec_ref, sdec_ref,
                o_ref, sout_ref, s_scr, *, c, cps, mm):
    ci = pl.program_id(1)

    @pl.when(ci == 0)
    def _():
        s_scr[...] = s0_ref[0].reshape(s_scr.shape)

    head_of_lane = lax.broadcasted_iota(jnp.int32, (c, QR_W), 1) // DK_R
    for step in range(cps):
        r0 = step * c
        q = q_ref[0, r0:r0 + c, :]
        k = k_ref[0, r0:r0 + c, :]
        v = v_ref[0, r0:r0 + c, :].astype(mm)
        gate = g_ref[0, r0:r0 + c, :]
        s_old = s_scr[...]
        s_mm = s_old.astype(mm)
        k_mm = k.astype(mm)
        kd = (k.astype(F32) * kdec_ref[...]).astype(mm)
        upd = lax.dot_general(kd, v, _TN, preferred_element_type=F32)
        for h in range(H_R):
            qh = jnp.where(head_of_lane == h, q, jnp.zeros_like(q)).astype(mm)
            vh = v[:, h * DV_R:(h + 1) * DV_R]
            s = lax.dot_general(qh, k_mm, _NT, preferred_element_type=F32) * dmat_ref[h]
            intra = jnp.dot(s.astype(mm), vh, preferred_element_type=F32)
            cross = jnp.dot(qh, s_mm, preferred_element_type=F32) * qdec_ref[:, h * DV_R:(h + 1) * DV_R]
            gh = gate[:, h * DV_R:(h + 1) * DV_R]
            y = _rms(intra + cross) * (gh * jax.nn.sigmoid(gh))
            o_ref[0, r0:r0 + c, h * DV_R:(h + 1) * DV_R] = y.astype(o_ref.dtype)
        diag = [upd[h * DK_R:(h + 1) * DK_R, h * DV_R:(h + 1) * DV_R] for h in range(H_R)]
        s_scr[...] = s_old * sdec_ref[...] + jnp.concatenate(diag, axis=0)

    @pl.when(ci == pl.num_programs(1) - 1)
    def _():
        sout_ref[0] = s_scr[...].reshape(sout_ref.shape[1:])


def _retention(qr, kr, vr, gr, s0, tabs, *, c, cps, mm, out_dtype):
    b, t, _ = qr.shape
    dmat, qdec, kdec, sdec = tabs
    tt = c * cps
    blk = lambda bi, ci: (bi, ci, 0)
    st = lambda bi, ci: (bi, 0, 0, 0)
    return pl.pallas_call(
        functools.partial(_ret_kernel, c=c, cps=cps, mm=mm),
        grid=(b, t // tt),
        in_specs=[pl.BlockSpec((1, tt, QR_W), blk), pl.BlockSpec((1, tt, KR_W), blk),
                  pl.BlockSpec((1, tt, VR_W), blk), pl.BlockSpec((1, tt, GR_W), blk),
                  pl.BlockSpec((1, H_R, DK_R, DV_R), st),
                  _const_spec(dmat.shape), _const_spec(qdec.shape), _const_spec(kdec.shape),
                  _const_spec(sdec.shape)],
        out_specs=[pl.BlockSpec((1, tt, VR_W), blk), pl.BlockSpec((1, H_R, DK_R, DV_R), st)],
        out_shape=[jax.ShapeDtypeStruct((b, t, VR_W), out_dtype),
                   jax.ShapeDtypeStruct((b, H_R, DK_R, DV_R), F32)],
        scratch_shapes=[pltpu.VMEM((H_R * DK_R, DV_R), F32)],
        compiler_params=_params("parallel", "arbitrary"), name="retention",
    )(qr, kr, vr, gr, s0, dmat, qdec, kdec, sdec)


def _mix_kernel(oa_ref, or_ref, h_ref, wo_ref, gpost_ref, gpre_ref, wq_ref, h1_ref, qm_ref, *, dh_m):
    mix_in = jnp.concatenate([oa_ref[...], or_ref[...]], axis=-1).astype(BF16)
    mix = jnp.dot(mix_in, wo_ref[...], preferred_element_type=F32)
    h1 = h_ref[...] + _rms(mix, gpost_ref[...])
    h1_ref[...] = h1
    xn = _rms(h1, gpre_ref[...]).astype(BF16)
    qm_ref[...] = (jnp.dot(xn, wq_ref[...], preferred_element_type=F32) * dh_m ** -0.5).astype(qm_ref.dtype)


def _mix(oa, orr, h, w_out, g_post, g_pre, w_mq, *, tm, act_dtype):
    n, d = h.shape
    row = lambda i: (i, 0)
    return pl.pallas_call(
        functools.partial(_mix_kernel, dh_m=d // H_M),
        grid=(n // tm,),
        in_specs=[pl.BlockSpec((tm, VA_W), row), pl.BlockSpec((tm, VR_W), row), pl.BlockSpec((tm, d), row),
                  _const_spec(w_out.shape), _const_spec((1, d)), _const_spec((1, d)), _const_spec(w_mq.shape)],
        out_specs=[pl.BlockSpec((tm, d), row), pl.BlockSpec((tm, d), row)],
        out_shape=[jax.ShapeDtypeStruct((n, d), F32), jax.ShapeDtypeStruct((n, d), act_dtype)],
        compiler_params=_params("parallel"), name="mix_out",
    )(oa, orr, h, w_out, g_post, g_pre, w_mq)


def _memkv_kernel(mem_ref, g_ref, wk_ref, wv_ref, mk_ref, mv_ref):
    mn = _rms(mem_ref[...], g_ref[...]).astype(BF16)
    mk_ref[...] = jnp.dot(mn, wk_ref[...], preferred_element_type=F32)
    mv_ref[...] = jnp.dot(mn, wv_ref[...], preferred_element_type=F32)


def _memkv(mem2d, g, w_mk, w_mv, *, tm):
    n, d = mem2d.shape
    row = lambda i: (i, 0)
    return pl.pallas_call(
        _memkv_kernel, grid=(n // tm,),
        in_specs=[pl.BlockSpec((tm, d), row), _const_spec((1, d)), _const_spec(w_mk.shape), _const_spec(w_mv.shape)],
        out_specs=[pl.BlockSpec((tm, d), row)] * 2,
        out_shape=[jax.ShapeDtypeStruct((n, d), F32)] * 2,
        compiler_params=_params("parallel"), name="mem_kv",
    )(mem2d, g, w_mk, w_mv)


def _memattn_kernel(q_ref, mk_ref, mv_ref, o_ref, *, mm):
    q = q_ref[0]
    dh = q.shape[-1] // H_M
    for h in range(H_M):
        qh = q[:, h * dh:(h + 1) * dh].astype(mm)
        kh = mk_ref[0, :, h * dh:(h + 1) * dh].astype(mm)
        vh = mv_ref[0, :, h * dh:(h + 1) * dh].astype(mm)
        s = lax.dot_general(qh, kh, _NT, preferred_element_type=F32)
        p = jnp.exp(s - jnp.max(s, axis=-1, keepdims=True))
        p = p * (1.0 / jnp.sum(p, axis=-1, keepdims=True))
        o_ref[0, :, h * dh:(h + 1) * dh] = jnp.dot(p.astype(mm), vh, preferred_element_type=F32).astype(o_ref.dtype)


def _memattn(qm, mk, mv, *, tq, mm):
    b, t, d = qm.shape
    n_mem = mk.shape[1]
    return pl.pallas_call(
        functools.partial(_memattn_kernel, mm=mm),
        grid=(b, t // tq),
        in_specs=[pl.BlockSpec((1, tq, d), lambda bi, qi: (bi, qi, 0)),
                  pl.BlockSpec((1, n_mem, d), lambda bi, qi: (bi, 0, 0)),
                  pl.BlockSpec((1, n_mem, d), lambda bi, qi: (bi, 0, 0))],
        out_specs=pl.BlockSpec((1, tq, d), lambda bi, qi: (bi, qi, 0)),
        out_shape=jax.ShapeDtypeStruct((b, t, d), qm.dtype),
        compiler_params=_params("parallel", "arbitrary"), name="mem_attn",
    )(qm, mk, mv)


def _ffn_kernel(h1_ref, o_ref, wmo_ref, gpm_ref, gpre_ref, gpost_ref, wg_ref, wu_ref, cw_ref, cb_ref, wd_ref,
                *rest, tm, seq, fw, carry):
    if carry:
        prev_ref, y_ref, cnew_ref, a_scr, h_scr = rest
    else:
        p1_ref, p2_ref, y_ref, a_ref, a_scr, h_scr = rest
    i = pl.program_id(0)
    pad = SUBLANES
    m = jnp.dot(o_ref[...].astype(BF16), wmo_ref[...], preferred_element_type=F32)
    h2 = h1_ref[...] + _rms(m, gpm_ref[...])
    xn = _rms(h2, gpre_ref[...]).astype(BF16)
    d_ff = wg_ref.shape[1]

    if carry:
        @pl.when(i % (seq // tm) == 0)
        def _():
            a_scr[pad - 2:pad, :] = prev_ref[0]
    else:
        a_scr[0:pad, :] = jnp.zeros((pad, d_ff), F32)
        tpos = lax.broadcasted_iota(jnp.int32, (tm, fw), 0) % seq

    for c0 in range(0, d_ff, fw):
        cs = slice(c0, c0 + fw)
        a = jnp.dot(xn, wg_ref[:, cs], preferred_element_type=F32)
        a_scr[pad:pad + tm, cs] = a
        s1 = a_scr[pad - 1:pad - 1 + tm, cs]
        s2 = a_scr[pad - 2:pad - 2 + tm, cs]
        if not carry:
            s1 = jnp.where(tpos == 0, p1_ref[:, cs], s1)
            s2 = jnp.where(tpos < 2, p2_ref[:, cs], s2)
        conv = cb_ref[:, cs] + s2 * cw_ref[0:1, cs] + s1 * cw_ref[1:2, cs] + a * cw_ref[2:3, cs]
        up = jnp.dot(xn, wu_ref[:, cs], preferred_element_type=F32)
        h_scr[:, cs] = (_gelu_exact(conv) * up).astype(BF16)

    if carry:
        cnew_ref[0] = a_scr[pad + tm - 2:pad + tm, :]
        a_scr[0:pad, :] = a_scr[tm:tm + pad, :]
    else:
        a_ref[...] = a_scr[pad:pad + tm, :]
    f = jnp.dot(h_scr[...], wd_ref[...], preferred_element_type=F32)
    y_ref[...] = h2 + _rms(f, gpost_ref[...])


def _ffn(h1, o, w_mo, g_post_mem, g_pre, g_post, w_gate, w_up, conv_w, conv_b, w_down, prev, *, tm, seq, fw, carry):
    n, d = h1.shape
    d_ff = w_gate.shape[1]
    row = lambda i: (i, 0)
    in_specs = [pl.BlockSpec((tm, d), row), pl.BlockSpec((tm, d), row), _const_spec(w_mo.shape),
                _const_spec((1, d)), _const_spec((1, d)), _const_spec((1, d)),
                _const_spec(w_gate.shape), _const_spec(w_up.shape), _const_spec(conv_w.shape),
                _const_spec((1, d_ff)), _const_spec(w_down.shape)]
    out_specs = [pl.BlockSpec((tm, d), row)]
    out_shape = [jax.ShapeDtypeStruct((n, d), F32)]
    if carry:
        tps = seq // tm
        in_specs += [pl.BlockSpec((1, CONV_W - 1, d_ff), lambda i: (i // tps, 0, 0))]
        out_specs += [pl.BlockSpec((1, CONV_W - 1, d_ff), lambda i: (i // tps, 0, 0))]
        out_shape += [jax.ShapeDtypeStruct((n // seq, CONV_W - 1, d_ff), F32)]
        extra = (prev,)
    else:
        in_specs += [pl.BlockSpec((tm, d_ff), row)] * 2
        out_specs += [pl.BlockSpec((tm, d_ff), row)]
        out_shape += [jax.ShapeDtypeStruct((n, d_ff), F32)]
        extra = prev
    return pl.pallas_call(
        functools.partial(_ffn_kernel, tm=tm, seq=seq, fw=fw, carry=carry),
        grid=(n // tm,), in_specs=in_specs, out_specs=out_specs, out_shape=out_shape,
        scratch_shapes=[pltpu.VMEM((tm + SUBLANES, d_ff), F32), pltpu.VMEM((tm, d_ff), BF16)],
        compiler_params=_params("arbitrary"), name="mem_out_ffn",
    )(h1, o, w_mo, g_post_mem, g_pre, g_post, w_gate, w_up, conv_w, conv_b, w_down, *extra)


def _rotary_tables(pos, k_scale):
    half = DK_R // 2
    inv = 1.0 / (ROPE_BASE ** jnp.linspace(0.0, 1.0, half, dtype=F32))
    ang = pos.astype(F32)[:, None] * inv[None, :]
    cos, sin = jnp.cos(ang), jnp.sin(ang)
    cos_t = jnp.tile(jnp.concatenate([cos, cos], axis=-1), (1, H_R))
    sin_t = jnp.tile(jnp.concatenate([-sin, sin], axis=-1), (1, H_R))
    return cos_t, sin_t, cos_t * k_scale, sin_t * k_scale


def _retention_tables(c):
    log_g = jnp.log(1.0 - 2.0 ** (-5.0 - jnp.arange(H_R, dtype=F32)))
    i = jnp.arange(c, dtype=F32)
    diff = i[:, None] - i[None, :]
    dmat = jnp.where(diff[None] >= 0, jnp.exp(jnp.maximum(diff, 0.0)[None] * log_g[:, None, None]), 0.0)
    qdec = jnp.exp((i + 1.0)[:, None] * log_g[None, :])
    kdec = jnp.exp((c - 1.0 - i)[:, None] * log_g[None, :])
    sdec = jnp.exp(c * log_g)
    return (dmat, jnp.repeat(qdec, DV_R, axis=1), jnp.repeat(kdec, DK_R, axis=1),
            jnp.broadcast_to(jnp.repeat(sdec, DK_R)[:, None], (H_R * DK_R, DV_R)))


def _pick(n, pref):
    t = min(n, pref)
    while n % t:
        t //= 2
    return t


def kernel(x_prompt, x_sample, cache_k, cache_v, cache_mem_k, cache_mem_v, state_ret, state_conv, page_table, mem_prompt, w_in, w_out, lam_q1, lam_k1, lam_q2, lam_k2, g_diff_head, g_pre_mix, g_post_mix, g_pre_mem, g_post_mem, g_mem_in, w_mq, w_mk, w_mv, w_mo, g_pre_ffn, g_post_ffn, w_gate, w_up, conv_w, conv_b, w_down):
    b, s, d = x_prompt.shape
    bd, t, _ = x_sample.shape
    depth = w_in.shape[0]
    n_mem = mem_prompt.shape[1]
    d_ff = w_gate.shape[-1]
    past = page_table.shape[1] * PAGE_SIZE
    n_pool = cache_k.shape[1]
    assert depth == 1, "the sample path keeps the layer loop unrolled for a single layer"
    l = 0
    lam_init = 0.8 - 0.6 * math.exp(-0.3 * l)

    row = lambda v: v[l].reshape(1, -1)
    bf = lambda w: w[l].astype(BF16)
    w_in_b, w_out_b, w_mq_b, w_mk_b, w_mv_b, w_mo_b = map(bf, (w_in, w_out, w_mq, w_mk, w_mv, w_mo))
    w_gate_b, w_up_b, w_down_b = map(bf, (w_gate, w_up, w_down))
    lams = tuple(row(v) for v in (lam_q1, lam_k1, lam_q2, lam_k2))
    g_head = g_diff_head[l]

    tm_p = _pick(s, 512)
    tkv = _pick(s, 256)
    tq = _pick(s, 256)
    ret_c = RET_CHUNK if s % RET_CHUNK == 0 else s
    ret_c_s = RET_CHUNK if t % RET_CHUNK == 0 else t
    n_s = bd * t

    mk_p, mv_p = _memkv(mem_prompt.reshape(b * n_mem, d), row(g_mem_in), w_mk_b, w_mv_b, tm=_pick(b * n_mem, 512))
    tabs_p = _rotary_tables(jnp.arange(s), DK_R ** -0.5)
    qa, ka, va, kab, vat, qr, kr, vr, gr = _inproj(
        x_prompt.reshape(b * s, d), row(g_pre_mix), w_in_b, tabs_p, tm=tm_p, with_vt=True, tkv=tkv, act_dtype=BF16)
    oa = _dattn_prompt(lams, g_head.reshape(DV_A, 1), qa, kab, vat, b=b, s=s, tq=tq, tk=tkv, lam_init=lam_init)
    r3 = lambda a: a.reshape(b, s, a.shape[-1])
    orr, ret_p = _retention(r3(qr), r3(kr), r3(vr), r3(gr), jnp.zeros((b, H_R, DK_R, DV_R), F32),
                            _retention_tables(ret_c), c=ret_c, cps=_pick(s // ret_c, 4), mm=BF16, out_dtype=BF16)
    h1, qm = _mix(oa, orr.reshape(b * s, VR_W), x_prompt.reshape(b * s, d), w_out_b, row(g_post_mix),
                  row(g_pre_mem), w_mq_b, tm=tm_p, act_dtype=BF16)
    om = _memattn(qm.reshape(b, s, d), mk_p.reshape(b, n_mem, d), mv_p.reshape(b, n_mem, d), tq=tm_p, mm=BF16)
    tm_f = _pick(s, 256)
    y_p, conv_p = _ffn(h1, om.reshape(b * s, d), w_mo_b, row(g_post_mem), row(g_pre_ffn), row(g_post_ffn),
                       w_gate_b, w_up_b, conv_w[l], row(conv_b), w_down_b,
                       jnp.zeros((b, CONV_W - 1, d_ff), F32), tm=tm_f, seq=s, fw=d_ff // 2, carry=True)

    tm_s = _pick(n_s, 512)
    reps = tm_s // t
    tabs_s = tuple(jnp.tile(a, (reps, 1)) for a in _rotary_tables(past + jnp.arange(t), DK_R ** -0.5))
    qa_s, ka_s, va_s, qr_s, kr_s, vr_s, gr_s = _inproj(
        x_sample.reshape(n_s, d), row(g_pre_mix), w_in_b, tabs_s, tm=tm_s, with_vt=False, tkv=tkv, act_dtype=F32)
    b3 = lambda a: a.reshape(bd, t, a.shape[-1])
    oa_s = _dattn_sample(page_table, lams, g_head.reshape(1, DV_A), b3(qa_s), b3(ka_s), b3(va_s),
                         cache_k[l].reshape(n_pool, PAGE_SIZE * H_A, 2 * DH_A),
                         cache_v[l].reshape(n_pool, PAGE_SIZE * H_A, DV_A),
                         pp=_pick(page_table.shape[1], 8), lam_init=lam_init)
    orr_s, ret_s = _retention(b3(qr_s), b3(kr_s), b3(vr_s), b3(gr_s), state_ret[l], _retention_tables(ret_c_s),
                              c=ret_c_s, cps=_pick(t // ret_c_s, 4), mm=F32, out_dtype=F32)
    h1_s, qm_s = _mix(oa_s.reshape(n_s, VA_W), orr_s.reshape(n_s, VR_W), x_sample.reshape(n_s, d), w_out_b,
                      row(g_post_mix), row(g_pre_mem), w_mq_b, tm=tm_s, act_dtype=F32)
    om_s = _memattn(b3(qm_s), cache_mem_k[l].reshape(bd, n_mem, d), cache_mem_v[l].reshape(bd, n_mem, d),
                    tq=t, mm=F32)
    prev = state_conv[l]
    zeros = jnp.zeros((bd, t - 2, d_ff), F32)
    p1 = jnp.concatenate([prev[:, 1:2], jnp.zeros((bd, 1, d_ff), F32), zeros], axis=1).reshape(n_s, d_ff)
    p2 = jnp.concatenate([prev, zeros], axis=1).reshape(n_s, d_ff)
    tm_fs = _pick(n_s, 256)
    y_s, a_s = _ffn(h1_s, om_s.reshape(n_s, d), w_mo_b, row(g_post_mem), row(g_pre_ffn), row(g_post_ffn),
                    w_gate_b, w_up_b, conv_w[l], row(conv_b), w_down_b, (p1, p2),
                    tm=tm_fs, seq=t, fw=d_ff // 2, carry=False)
    conv_s = a_s.reshape(bd, t, d_ff)[:, t - (CONV_W - 1):]

    return (y_p.reshape(b, s, d), y_s.reshape(bd, t, d),
            ka.reshape(1, b, s, H_A, 2 * DH_A), va.reshape(1, b, s, H_A, DV_A),
            mk_p.reshape(1, b, n_mem, H_M, d // H_M), mv_p.reshape(1, b, n_mem, H_M, d // H_M),
            ret_p[None], conv_p[None],
            ka_s.reshape(1, bd, t, H_A, 2 * DH_A), va_s.reshape(1, bd, t, H_A, DV_A),
            ret_s[None], conv_s[None])
```

```python
import functools
import math

import jax
import jax.numpy as jnp
from jax import lax
from jax.experimental import pallas as pl
from jax.experimental.pallas import tpu as pltpu

F32 = jnp.float32
BF16 = jnp.bfloat16

H_A = 4
DH_A = 64
DV_A = 2 * DH_A
H_R = 4
DK_R = 64
DV_R = 128
RET_CHUNK = 128
ROPE_BASE = 10000.0
H_M = 4
CONV_W = 3
PAGE_SIZE = 128
EPS = 1e-6
NEG = -1e30

QA_W = H_A * 2 * DH_A
KA_W = H_A * 2 * DH_A
VA_W = H_A * DV_A
QR_W = H_R * DK_R
KR_W = H_R * DK_R
VR_W = H_R * DV_R
GR_W = H_R * DV_R

VMEM_LIMIT = 56 * 1024 * 1024
SUBLANES = 8
LANES = 128

_NT = (((1,), (1,)), ((), ()))
_TN = (((0,), (0,)), ((), ()))


def _params(*sem):
    return pltpu.CompilerParams(dimension_semantics=sem, vmem_limit_bytes=VMEM_LIMIT)


def _const_spec(shape):
    nd = len(shape)
    return pl.BlockSpec(shape, lambda *_: (0,) * nd, pipeline_mode=pl.Buffered(1))


def _rms(x, g=None):
    y = x * lax.rsqrt(jnp.mean(x * x, axis=-1, keepdims=True) + EPS)
    return y if g is None else y * g


def _gelu_exact(x):
    return 0.5 * x * (1.0 + lax.erf(x * (2.0 ** -0.5)))


def _diff_lambda(lq1, lk1, lq2, lk2, lam_init):
    a = jnp.sum(lq1[...] * lk1[...], axis=-1, keepdims=True)
    b = jnp.sum(lq2[...] * lk2[...], axis=-1, keepdims=True)
    return jnp.exp(a) - jnp.exp(b) + lam_init


def _rotate(x, cos_t, sin_t):
    half = DK_R // 2
    w = x.shape[-1]
    lane = lax.broadcasted_iota(jnp.int32, x.shape, 1)
    swapped = jnp.where(lane % DK_R < half, pltpu.roll(x, w - half, 1), pltpu.roll(x, half, 1))
    return x * cos_t + swapped * sin_t


def _inproj_kernel(x_ref, g_ref, w_ref, cq_ref, sq_ref, ck_ref, sk_ref, *out_refs, prompt, tkv, q_scale):
    if prompt:
        qa_ref, ka_ref, va_ref, kab_ref, vat_ref, qr_ref, kr_ref, vr_ref, gr_ref = out_refs
    else:
        qa_ref, ka_ref, va_ref, qr_ref, kr_ref, vr_ref, gr_ref = out_refs
    xn = _rms(x_ref[...], g_ref[...]).astype(BF16)
    tm = xn.shape[0]

    def proj(c0, width):
        return jnp.dot(xn, w_ref[:, c0:c0 + width], preferred_element_type=F32)

    def store_heads(ref, val, width):
        if prompt:
            for h in range(H_A):
                ref[pl.ds(h, tm, stride=H_A), :] = val[:, h * width:(h + 1) * width]
        else:
            ref[...] = val

    c = 0
    qa_ref[...] = (proj(c, QA_W) * q_scale).astype(qa_ref.dtype)
    c += QA_W
    ka = proj(c, KA_W)
    store_heads(ka_ref, ka, 2 * DH_A)
    c += KA_W
    va = proj(c, VA_W)
    store_heads(va_ref, va, DV_A)
    c += VA_W
    if prompt:
        kab_ref[...] = ka.astype(BF16)
        for r in range(tm // tkv):
            vat_ref[r] = va[r * tkv:(r + 1) * tkv, :].T.astype(BF16)
    qr_ref[...] = _rotate(proj(c, QR_W), cq_ref[...], sq_ref[...]).astype(qr_ref.dtype)
    c += QR_W
    kr_ref[...] = _rotate(proj(c, KR_W), ck_ref[...], sk_ref[...]).astype(kr_ref.dtype)
    c += KR_W
    vr_ref[...] = proj(c, VR_W).astype(vr_ref.dtype)
    c += VR_W
    gr_ref[...] = proj(c, GR_W)


def _inproj(x2d, g, w_in, tabs, *, tm, prompt, tkv, q_scale, act_dtype):
    n, d = x2d.shape
    period = tabs[0].shape[0]
    npb = period // tm
    row = lambda i: (i, 0)
    tab = lambda i: (i % npb, 0)
    in_specs = [pl.BlockSpec((tm, d), row), _const_spec((1, d)), _const_spec(w_in.shape)]
    in_specs += [pl.BlockSpec((tm, QR_W), tab)] * 4
    out_shape = [jax.ShapeDtypeStruct((n, QA_W), act_dtype)]
    out_specs = [pl.BlockSpec((tm, QA_W), row)]
    if prompt:
        out_shape += [jax.ShapeDtypeStruct((n * H_A, 2 * DH_A), F32),
                      jax.ShapeDtypeStruct((n * H_A, DV_A), F32),
                      jax.ShapeDtypeStruct((n, KA_W), BF16),
                      jax.ShapeDtypeStruct((n // tkv, VA_W, tkv), BF16)]
        out_specs += [pl.BlockSpec((tm * H_A, 2 * DH_A), row), pl.BlockSpec((tm * H_A, DV_A), row),
                      pl.BlockSpec((tm, KA_W), row),
                      pl.BlockSpec((tm // tkv, VA_W, tkv), lambda i: (i, 0, 0))]
    else:
        out_shape += [jax.ShapeDtypeStruct((n, KA_W), F32), jax.ShapeDtypeStruct((n, VA_W), F32)]
        out_specs += [pl.BlockSpec((tm, KA_W), row), pl.BlockSpec((tm, VA_W), row)]
    out_shape += [jax.ShapeDtypeStruct((n, QR_W), act_dtype),
                  jax.ShapeDtypeStruct((n, KR_W), act_dtype),
                  jax.ShapeDtypeStruct((n, VR_W), act_dtype),
                  jax.ShapeDtypeStruct((n, GR_W), F32)]
    out_specs += [pl.BlockSpec((tm, QR_W), row), pl.BlockSpec((tm, KR_W), row),
                  pl.BlockSpec((tm, VR_W), row), pl.BlockSpec((tm, GR_W), row)]
    return pl.pallas_call(
        functools.partial(_inproj_kernel, prompt=prompt, tkv=tkv, q_scale=q_scale),
        grid=(n // tm,), in_specs=in_specs, out_specs=out_specs, out_shape=out_shape,
        compiler_params=_params("parallel"), name="inproj",
    )(x2d, g, w_in, *tabs)


def _dattn_p_kernel(lq1, lk1, lq2, lk2, g_ref, q_ref, k_ref, vt_ref, o_ref, qbd_scr, st_scr, p_scr, acc_scr,
                    *, tq, tk, lam_init):
    qi = pl.program_id(2)
    ndiag = tq // tk
    nfull = qi * ndiag
    q = q_ref[...]
    lane = lax.broadcasted_iota(jnp.int32, q.shape, 1)
    zero = jnp.zeros_like(q)
    qbd_scr[0:tq, :] = jnp.where(lane < DH_A, q, zero)
    qbd_scr[tq:2 * tq, :] = jnp.where(lane >= DH_A, q, zero)
    acc_scr[...] = jnp.zeros(acc_scr.shape, F32)
    p_scr[1] = jnp.zeros(p_scr.shape[1:], BF16)

    gw = 2 * LANES
    ngroup = 2 * tq // gw

    def scores(j, slot, g):
        kj = k_ref[pl.ds(pl.multiple_of(j * tk, tk), tk), :]
        st_scr[slot, :, g * gw:(g + 1) * gw] = lax.dot_general(
            kj, qbd_scr[g * gw:(g + 1) * gw, :], _NT, preferred_element_type=F32)

    def block(j, slot, m, l, masked, issue_next):
        vt_prev = vt_ref[jnp.maximum(j - 1, 0)]
        m_out, l_out = [], []
        for g in range(ngroup):
            if issue_next:
                scores(j + 1, 1 - slot, g)
            pv = jnp.dot(vt_prev, p_scr[1 - slot, :, g * gw:(g + 1) * gw], preferred_element_type=F32)
            alphas = []
            for c in range(g * gw // LANES, (g + 1) * gw // LANES):
                cs = slice(c * LANES, (c + 1) * LANES)
                st = st_scr[slot, :, cs]
                if masked:
                    kpos = j * tk + lax.broadcasted_iota(jnp.int32, st.shape, 0)
                    qpos = qi * tq + (c * LANES) % tq + lax.broadcasted_iota(jnp.int32, st.shape, 1)
                    st = jnp.where(kpos <= qpos, st, NEG)
                m_new = jnp.maximum(m[:, cs], jnp.max(st, axis=0, keepdims=True))
                alpha = jnp.exp2(m[:, cs] - m_new)
                p = jnp.exp2(st - m_new)
                l_out.append(alpha * l[:, cs] + jnp.sum(p, axis=0, keepdims=True))
                m_out.append(m_new)
                alphas.append(alpha)
                p_scr[slot, :, cs] = p.astype(BF16)
            gs = slice(g * gw, (g + 1) * gw)
            acc_scr[:, gs] = (acc_scr[:, gs] + pv) * jnp.concatenate(alphas, axis=-1)
        return jnp.concatenate(m_out, axis=-1), jnp.concatenate(l_out, axis=-1)

    def pair(i, carry):
        m, l = block(2 * i, 0, carry[0], carry[1], False, True)
        return block(2 * i + 1, 1, m, l, False, True)

    for g in range(ngroup):
        scores(0, 0, g)
    carry = (jnp.full((1, 2 * tq), NEG, F32), jnp.zeros((1, 2 * tq), F32))
    carry = lax.fori_loop(0, qi, pair, carry)
    carry = block(nfull, 0, carry[0], carry[1], True, True)
    carry = block(nfull + 1, 1, carry[0], carry[1], True, False)
    acc = acc_scr[...] + jnp.dot(vt_ref[nfull + 1], p_scr[1], preferred_element_type=F32)
    lam = _diff_lambda(lq1, lk1, lq2, lk2, lam_init)
    o = acc * (1.0 / carry[1])
    o = o[:, :tq] - lam * o[:, tq:]
    y = o * lax.rsqrt(jnp.mean(o * o, axis=0, keepdims=True) + EPS) * g_ref[...] * (1.0 - lam_init)
    o_ref[...] = y.T.astype(o_ref.dtype)


def _dattn_prompt(lams, g_col, qa, kab, vat, *, b, s, tq, tk, lam_init):
    n = qa.shape[0]
    hw = 2 * DH_A
    nq = s // tq
    assert tq == 2 * tk and s % tq == 0, "the kernel walks key blocks in pairs"
    lam_specs = [_const_spec((1, DH_A))] * 4
    return pl.pallas_call(
        functools.partial(_dattn_p_kernel, tq=tq, tk=tk, lam_init=lam_init),
        grid=(b, H_A, nq),
        in_specs=lam_specs + [
            _const_spec((DV_A, 1)),
            pl.BlockSpec((tq, hw), lambda bi, h, qi: (bi * nq + qi, h)),
            pl.BlockSpec((s, hw), lambda bi, h, qi: (bi, h)),
            pl.BlockSpec((s // tk, DV_A, tk), lambda bi, h, qi: (bi, h, 0)),
        ],
        out_specs=pl.BlockSpec((tq, DV_A), lambda bi, h, qi: (bi * nq + qi, h)),
        out_shape=jax.ShapeDtypeStruct((n, VA_W), BF16),
        scratch_shapes=[pltpu.VMEM((2 * tq, hw), BF16), pltpu.VMEM((2, tk, 2 * tq), F32),
                        pltpu.VMEM((2, tk, 2 * tq), BF16), pltpu.VMEM((DV_A, 2 * tq), F32)],
        compiler_params=_params("parallel", "parallel", "arbitrary"), name="dattn_prompt",
    )(*lams, g_col, qa, kab, vat)


def _dattn_s_kernel(pt_ref, lq1, lk1, lq2, lk2, g_ref, q_ref, kn_ref, vn_ref, ck_hbm, cv_hbm, o_ref,
                    kbuf, vbuf, sems, m_scr, l_scr, acc_scr, *, pp, t, lam_init):
    bi = pl.program_id(0)
    gi = pl.program_id(1)
    ng = pl.num_programs(1)
    nsteps = pl.num_programs(0) * ng
    step = bi * ng + gi
    slot = step % 2
    rows = PAGE_SIZE * H_A
    hm = 2 * t
    nkeys = pp * PAGE_SIZE

    def copies(step_, slot_):
        b_ = step_ // ng
        g_ = step_ % ng
        out = []
        for p in range(pp):
            page = pt_ref[b_, g_ * pp + p]
            out.append(pltpu.make_async_copy(ck_hbm.at[page], kbuf.at[slot_, pl.ds(p * rows, rows)], sems.at[0, slot_]))
            out.append(pltpu.make_async_copy(cv_hbm.at[page], vbuf.at[slot_, pl.ds(p * rows, rows)], sems.at[1, slot_]))
        return out

    @pl.when(step == 0)
    def _():
        for c in copies(step, slot):
            c.start()

    @pl.when(step + 1 < nsteps)
    def _():
        for c in copies(step + 1, 1 - slot):
            c.start()

    @pl.when(gi == 0)
    def _():
        m_scr[...] = jnp.full(m_scr.shape, NEG, F32)
        l_scr[...] = jnp.zeros(l_scr.shape, F32)
        acc_scr[...] = jnp.zeros(acc_scr.shape, F32)

    q = q_ref[0]
    qbd = []
    for h in range(H_A):
        qh = q[:, h * 2 * DH_A:(h + 1) * 2 * DH_A]
        lane = lax.broadcasted_iota(jnp.int32, qh.shape, 1)
        qbd.append(jnp.concatenate([jnp.where(lane < DH_A, qh, 0.0), jnp.where(lane >= DH_A, qh, 0.0)], axis=0))

    def update(s_all, v_heads):
        m = m_scr[...]
        m_new = jnp.maximum(m, jnp.max(s_all, axis=-1, keepdims=True))
        alpha = jnp.exp(m - m_new)
        p = jnp.exp(s_all - m_new)
        l_scr[...] = alpha * l_scr[...] + jnp.sum(p, axis=-1, keepdims=True)
        m_scr[...] = m_new
        pv = [jnp.dot(p[h * hm:(h + 1) * hm].astype(v_heads[h].dtype), v_heads[h], preferred_element_type=F32)
              for h in range(H_A)]
        acc_scr[...] = acc_scr[...] * alpha + jnp.concatenate(pv, axis=0)

    for c in copies(step, slot):
        c.wait()

    s_heads, v_heads = [], []
    for h in range(H_A):
        kh = kbuf[slot, pl.ds(h, nkeys, stride=H_A), :].astype(BF16)
        v_heads.append(vbuf[slot, pl.ds(h, nkeys, stride=H_A), :].astype(BF16))
        s_heads.append(lax.dot_general(qbd[h].astype(BF16), kh, _NT, preferred_element_type=F32))
    update(jnp.concatenate(s_heads, axis=0), v_heads)

    @pl.when(gi == ng - 1)
    def _():
        kn = kn_ref[0]
        vn = vn_ref[0]
        s_heads, v_heads = [], []
        for h in range(H_A):
            s_heads.append(lax.dot_general(qbd[h], kn[:, h * 2 * DH_A:(h + 1) * 2 * DH_A], _NT,
                                           preferred_element_type=F32))
            v_heads.append(vn[:, h * DV_A:(h + 1) * DV_A])
        s_new = jnp.concatenate(s_heads, axis=0)
        kidx = lax.broadcasted_iota(jnp.int32, s_new.shape, 1)
        qidx = lax.broadcasted_iota(jnp.int32, s_new.shape, 0) % t
        update(jnp.where(kidx <= qidx, s_new, NEG), v_heads)

        lam = _diff_lambda(lq1, lk1, lq2, lk2, lam_init)
        o = acc_scr[...] * (1.0 / l_scr[...])
        for h in range(H_A):
            oh = o[h * hm:h * hm + t] - lam * o[h * hm + t:(h + 1) * hm]
            y = _rms(oh, g_ref[...]) * (1.0 - lam_init)
            o_ref[0, :, h * DV_A:(h + 1) * DV_A] = y.astype(o_ref.dtype)


def _dattn_sample(page_table, lams, g_row, qa, kn, vn, ck, cv, *, pp, lam_init):
    bd, t, _ = qa.shape
    npages = page_table.shape[1]
    rows = PAGE_SIZE * H_A
    blk = lambda bi, gi, pt: (bi, 0, 0)
    cst = lambda bi, gi, pt: (0, 0)
    grid_spec = pltpu.PrefetchScalarGridSpec(
        num_scalar_prefetch=1,
        grid=(bd, npages // pp),
        in_specs=[pl.BlockSpec((1, DH_A), cst)] * 4 + [
            pl.BlockSpec((1, DV_A), cst),
            pl.BlockSpec((1, t, QA_W), blk),
            pl.BlockSpec((1, t, KA_W), blk),
            pl.BlockSpec((1, t, VA_W), blk),
            pl.BlockSpec(memory_space=pl.ANY),
            pl.BlockSpec(memory_space=pl.ANY),
        ],
        out_specs=pl.BlockSpec((1, t, VA_W), blk),
        scratch_shapes=[
            pltpu.VMEM((2, pp * rows, 2 * DH_A), F32),
            pltpu.VMEM((2, pp * rows, DV_A), F32),
            pltpu.SemaphoreType.DMA((2, 2)),
            pltpu.VMEM((H_A * 2 * t, 1), F32),
            pltpu.VMEM((H_A * 2 * t, 1), F32),
            pltpu.VMEM((H_A * 2 * t, DV_A), F32),
        ],
    )
    return pl.pallas_call(
        functools.partial(_dattn_s_kernel, pp=pp, t=t, lam_init=lam_init),
        grid_spec=grid_spec,
        out_shape=jax.ShapeDtypeStruct((bd, t, VA_W), F32),
        compiler_params=_params("arbitrary", "arbitrary"), name="dattn_sample",
    )(page_table, *lams, g_row, qa, kn, vn, ck, cv)


def _ret_kernel(q_ref, k_ref, v_ref, g_ref, s0_ref, dmat_ref, qdec_ref, kdec_ref, sdec_ref,
                o_ref, sout_ref, s_scr, *, c, cps, mm):
    ci = pl.program_id(1)

    @pl.when(ci == 0)
    def _():
        s_scr[...] = s0_ref[0].reshape(s_scr.shape)

    head_of_lane = lax.broadcasted_iota(jnp.int32, (c, QR_W), 1) // DK_R
    for step in range(cps):
        r0 = step * c
        q = q_ref[0, r0:r0 + c, :]
        k = k_ref[0, r0:r0 + c, :]
        v = v_ref[0, r0:r0 + c, :].astype(mm)
        gate = g_ref[0, r0:r0 + c, :]
        s_old = s_scr[...]
        s_mm = s_old.astype(mm)
        k_mm = k.astype(mm)
        kd = (k.astype(F32) * kdec_ref[...]).astype(mm)
        upd = lax.dot_general(kd, v, _TN, preferred_element_type=F32)
        for h in range(H_R):
            qh = jnp.where(head_of_lane == h, q, jnp.zeros_like(q)).astype(mm)
            vh = v[:, h * DV_R:(h + 1) * DV_R]
            s = lax.dot_general(qh, k_mm, _NT, preferred_element_type=F32) * dmat_ref[h]
            intra = jnp.dot(s.astype(mm), vh, preferred_element_type=F32)
            cross = jnp.dot(qh, s_mm, preferred_element_type=F32) * qdec_ref[:, h * DV_R:(h + 1) * DV_R]
            gh = gate[:, h * DV_R:(h + 1) * DV_R]
            y = _rms(intra + cross) * (gh * jax.nn.sigmoid(gh))
            o_ref[0, r0:r0 + c, h * DV_R:(h + 1) * DV_R] = y.astype(o_ref.dtype)
        diag = [upd[h * DK_R:(h + 1) * DK_R, h * DV_R:(h + 1) * DV_R] for h in range(H_R)]
        s_scr[...] = s_old * sdec_ref[...] + jnp.concatenate(diag, axis=0)

    @pl.when(ci == pl.num_programs(1) - 1)
    def _():
        sout_ref[0] = s_scr[...].reshape(sout_ref.shape[1:])


def _retention(qr, kr, vr, gr, s0, tabs, *, c, cps, mm, out_dtype):
    b, t, _ = qr.shape
    dmat, qdec, kdec, sdec = tabs
    tt = c * cps
    blk = lambda bi, ci: (bi, ci, 0)
    st = lambda bi, ci: (bi, 0, 0, 0)
    return pl.pallas_call(
        functools.partial(_ret_kernel, c=c, cps=cps, mm=mm),
        grid=(b, t // tt),
        in_specs=[pl.BlockSpec((1, tt, QR_W), blk), pl.BlockSpec((1, tt, KR_W), blk),
                  pl.BlockSpec((1, tt, VR_W), blk), pl.BlockSpec((1, tt, GR_W), blk),
                  pl.BlockSpec((1, H_R, DK_R, DV_R), st),
                  _const_spec(dmat.shape), _const_spec(qdec.shape), _const_spec(kdec.shape),
                  _const_spec(sdec.shape)],
        out_specs=[pl.BlockSpec((1, tt, VR_W), blk), pl.BlockSpec((1, H_R, DK_R, DV_R), st)],
        out_shape=[jax.ShapeDtypeStruct((b, t, VR_W), out_dtype),
                   jax.ShapeDtypeStruct((b, H_R, DK_R, DV_R), F32)],
        scratch_shapes=[pltpu.VMEM((H_R * DK_R, DV_R), F32)],
        compiler_params=_params("parallel", "arbitrary"), name="retention",
    )(qr, kr, vr, gr, s0, dmat, qdec, kdec, sdec)


def _mix_kernel(oa_ref, or_ref, h_ref, wo_ref, gpost_ref, gpre_ref, wq_ref, h1_ref, qm_ref, *, dh_m):
    mix_in = jnp.concatenate([oa_ref[...], or_ref[...]], axis=-1).astype(BF16)
    mix = jnp.dot(mix_in, wo_ref[...], preferred_element_type=F32)
    h1 = h_ref[...] + _rms(mix, gpost_ref[...])
    h1_ref[...] = h1
    xn = _rms(h1, gpre_ref[...]).astype(BF16)
    qm_ref[...] = (jnp.dot(xn, wq_ref[...], preferred_element_type=F32) * dh_m ** -0.5).astype(qm_ref.dtype)


def _mix(oa, orr, h, w_out, g_post, g_pre, w_mq, *, tm, act_dtype):
    n, d = h.shape
    row = lambda i: (i, 0)
    return pl.pallas_call(
        functools.partial(_mix_kernel, dh_m=d // H_M),
        grid=(n // tm,),
        in_specs=[pl.BlockSpec((tm, VA_W), row), pl.BlockSpec((tm, VR_W), row), pl.BlockSpec((tm, d), row),
                  _const_spec(w_out.shape), _const_spec((1, d)), _const_spec((1, d)), _const_spec(w_mq.shape)],
        out_specs=[pl.BlockSpec((tm, d), row), pl.BlockSpec((tm, d), row)],
        out_shape=[jax.ShapeDtypeStruct((n, d), F32), jax.ShapeDtypeStruct((n, d), act_dtype)],
        compiler_params=_params("parallel"), name="mix_out",
    )(oa, orr, h, w_out, g_post, g_pre, w_mq)


def _mem_chunk(h, c):
    return c * H_M + h


def _memkv_kernel(mem_ref, g_ref, wk_ref, wv_ref, mk_ref, mv_ref):
    mn = _rms(mem_ref[...], g_ref[...]).astype(BF16)
    tm, d = mn.shape
    halves = d // H_M // LANES
    nchunk = H_M * halves
    for w_ref, o_ref in ((wk_ref, mk_ref), (wv_ref, mv_ref)):
        y = jnp.dot(mn, w_ref[...], preferred_element_type=F32)
        for h in range(H_M):
            for c in range(halves):
                col = (h * halves + c) * LANES
                o_ref[pl.ds(_mem_chunk(h, c), tm, stride=nchunk), :] = y[:, col:col + LANES]


def _memkv(mem2d, g, w_mk, w_mv, *, tm):
    n, d = mem2d.shape
    nchunk = d // LANES
    row = lambda i: (i, 0)
    return pl.pallas_call(
        _memkv_kernel, grid=(n // tm,),
        in_specs=[pl.BlockSpec((tm, d), row), _const_spec((1, d)), _const_spec(w_mk.shape), _const_spec(w_mv.shape)],
        out_specs=[pl.BlockSpec((tm * nchunk, LANES), row)] * 2,
        out_shape=[jax.ShapeDtypeStruct((n * nchunk, LANES), F32)] * 2,
        compiler_params=_params("parallel"), name="mem_kv",
    )(mem2d, g, w_mk, w_mv)


def _memattn_kernel(q_ref, mk_ref, mv_ref, o_ref, *, mm, nb):
    d = q_ref.shape[-1]
    dh = d // H_M
    halves = dh // LANES
    nchunk = H_M * halves
    n_mem = mk_ref.shape[1] // nchunk

    def head(ref, bi, h):
        parts = [ref[bi, pl.ds(_mem_chunk(h, c), n_mem, stride=nchunk), :] for c in range(halves)]
        return jnp.concatenate(parts, axis=-1).astype(mm)

    for bi in range(nb):
        q = q_ref[bi]
        for h in range(H_M):
            qh = q[:, h * dh:(h + 1) * dh].astype(mm)
            s = lax.dot_general(qh, head(mk_ref, bi, h), _NT, preferred_element_type=F32)
            p = jnp.exp(s - jnp.max(s, axis=-1, keepdims=True))
            p = p * (1.0 / jnp.sum(p, axis=-1, keepdims=True))
            o = jnp.dot(p.astype(mm), head(mv_ref, bi, h), preferred_element_type=F32)
            o_ref[bi, :, h * dh:(h + 1) * dh] = o.astype(o_ref.dtype)


def _memattn(qm, mk, mv, *, tq, nb, mm):
    b, t, d = qm.shape
    rows = mk.shape[1]
    return pl.pallas_call(
        functools.partial(_memattn_kernel, mm=mm, nb=nb),
        grid=(b // nb, t // tq),
        in_specs=[pl.BlockSpec((nb, tq, d), lambda bi, qi: (bi, qi, 0)),
                  pl.BlockSpec((nb, rows, LANES), lambda bi, qi: (bi, 0, 0)),
                  pl.BlockSpec((nb, rows, LANES), lambda bi, qi: (bi, 0, 0))],
        out_specs=pl.BlockSpec((nb, tq, d), lambda bi, qi: (bi, qi, 0)),
        out_shape=jax.ShapeDtypeStruct((b, t, d), qm.dtype),
        compiler_params=_params("parallel", "arbitrary"), name="mem_attn",
    )(qm, mk, mv)


def _ffn_kernel(h1_ref, o_ref, wmo_ref, gpm_ref, gpre_ref, gpost_ref, wg_ref, wu_ref, cw_ref, cb_ref, wd_ref,
                *rest, tm, seq, fw, carry):
    if carry:
        prev_ref, y_ref, cnew_ref, a_scr, h_scr = rest
    else:
        p1_ref, p2_ref, y_ref, a_ref, a_scr, h_scr = rest
    i = pl.program_id(0)
    pad = SUBLANES
    m = jnp.dot(o_ref[...].astype(BF16), wmo_ref[...], preferred_element_type=F32)
    h2 = h1_ref[...] + _rms(m, gpm_ref[...])
    xn = _rms(h2, gpre_ref[...]).astype(BF16)
    d_ff = wg_ref.shape[1]

    if carry:
        @pl.when(i % (seq // tm) == 0)
        def _():
            a_scr[pad - 2:pad, :] = prev_ref[0]
    else:
        a_scr[0:pad, :] = jnp.zeros((pad, d_ff), F32)
        tpos = lax.broadcasted_iota(jnp.int32, (tm, fw), 0) % seq

    for c0 in range(0, d_ff, fw):
        cs = slice(c0, c0 + fw)
        a = jnp.dot(xn, wg_ref[:, cs], preferred_element_type=F32)
        a_scr[pad:pad + tm, cs] = a
        s1 = a_scr[pad - 1:pad - 1 + tm, cs]
        s2 = a_scr[pad - 2:pad - 2 + tm, cs]
        if not carry:
            s1 = jnp.where(tpos == 0, p1_ref[:, cs], s1)
            s2 = jnp.where(tpos < 2, p2_ref[:, cs], s2)
        conv = cb_ref[:, cs] + s2 * cw_ref[0:1, cs] + s1 * cw_ref[1:2, cs] + a * cw_ref[2:3, cs]
        up = jnp.dot(xn, wu_ref[:, cs], preferred_element_type=F32)
        h_scr[:, cs] = (_gelu_exact(conv) * up).astype(BF16)

    if carry:
        cnew_ref[0] = a_scr[pad + tm - 2:pad + tm, :]
        a_scr[0:pad, :] = a_scr[tm:tm + pad, :]
    else:
        a_ref[...] = a_scr[pad:pad + tm, :]
    f = jnp.dot(h_scr[...], wd_ref[...], preferred_element_type=F32)
    y_ref[...] = h2 + _rms(f, gpost_ref[...])


def _ffn(h1, o, w_mo, g_post_mem, g_pre, g_post, w_gate, w_up, conv_w, conv_b, w_down, prev, *, tm, seq, fw, carry):
    n, d = h1.shape
    d_ff = w_gate.shape[1]
    row = lambda i: (i, 0)
    in_specs = [pl.BlockSpec((tm, d), row), pl.BlockSpec((tm, d), row), _const_spec(w_mo.shape),
                _const_spec((1, d)), _const_spec((1, d)), _const_spec((1, d)),
                _const_spec(w_gate.shape), _const_spec(w_up.shape), _const_spec(conv_w.shape),
                _const_spec((1, d_ff)), _const_spec(w_down.shape)]
    out_specs = [pl.BlockSpec((tm, d), row)]
    out_shape = [jax.ShapeDtypeStruct((n, d), F32)]
    if carry:
        tps = seq // tm
        in_specs += [pl.BlockSpec((1, CONV_W - 1, d_ff), lambda i: (i // tps, 0, 0))]
        out_specs += [pl.BlockSpec((1, CONV_W - 1, d_ff), lambda i: (i // tps, 0, 0))]
        out_shape += [jax.ShapeDtypeStruct((n // seq, CONV_W - 1, d_ff), F32)]
        extra = (prev,)
    else:
        in_specs += [pl.BlockSpec((tm, d_ff), row)] * 2
        out_specs += [pl.BlockSpec((tm, d_ff), row)]
        out_shape += [jax.ShapeDtypeStruct((n, d_ff), F32)]
        extra = prev
    return pl.pallas_call(
        functools.partial(_ffn_kernel, tm=tm, seq=seq, fw=fw, carry=carry),
        grid=(n // tm,), in_specs=in_specs, out_specs=out_specs, out_shape=out_shape,
        scratch_shapes=[pltpu.VMEM((tm + SUBLANES, d_ff), F32), pltpu.VMEM((tm, d_ff), BF16)],
        compiler_params=_params("arbitrary"), name="mem_out_ffn",
    )(h1, o, w_mo, g_post_mem, g_pre, g_post, w_gate, w_up, conv_w, conv_b, w_down, *extra)


def _rotary_tables(pos, k_scale):
    half = DK_R // 2
    inv = 1.0 / (ROPE_BASE ** jnp.linspace(0.0, 1.0, half, dtype=F32))
    ang = pos.astype(F32)[:, None] * inv[None, :]
    cos, sin = jnp.cos(ang), jnp.sin(ang)
    cos_t = jnp.tile(jnp.concatenate([cos, cos], axis=-1), (1, H_R))
    sin_t = jnp.tile(jnp.concatenate([-sin, sin], axis=-1), (1, H_R))
    return cos_t, sin_t, cos_t * k_scale, sin_t * k_scale


def _retention_tables(c):
    log_g = jnp.log(1.0 - 2.0 ** (-5.0 - jnp.arange(H_R, dtype=F32)))
    i = jnp.arange(c, dtype=F32)
    diff = i[:, None] - i[None, :]
    dmat = jnp.where(diff[None] >= 0, jnp.exp(jnp.maximum(diff, 0.0)[None] * log_g[:, None, None]), 0.0)
    qdec = jnp.exp((i + 1.0)[:, None] * log_g[None, :])
    kdec = jnp.exp((c - 1.0 - i)[:, None] * log_g[None, :])
    sdec = jnp.exp(c * log_g)
    return (dmat, jnp.repeat(qdec, DV_R, axis=1), jnp.repeat(kdec, DK_R, axis=1),
            jnp.broadcast_to(jnp.repeat(sdec, DK_R)[:, None], (H_R * DK_R, DV_R)))


def _pick(n, pref):
    t = min(n, pref)
    while n % t:
        t //= 2
    return t


def kernel(x_prompt, x_sample, cache_k, cache_v, cache_mem_k, cache_mem_v, state_ret, state_conv, page_table, mem_prompt, w_in, w_out, lam_q1, lam_k1, lam_q2, lam_k2, g_diff_head, g_pre_mix, g_post_mix, g_pre_mem, g_post_mem, g_mem_in, w_mq, w_mk, w_mv, w_mo, g_pre_ffn, g_post_ffn, w_gate, w_up, conv_w, conv_b, w_down):
    b, s, d = x_prompt.shape
    bd, t, _ = x_sample.shape
    depth = w_in.shape[0]
    n_mem = mem_prompt.shape[1]
    d_ff = w_gate.shape[-1]
    past = page_table.shape[1] * PAGE_SIZE
    n_pool = cache_k.shape[1]
    assert depth == 1, "the sample path keeps the layer loop unrolled for a single layer"
    l = 0
    lam_init = 0.8 - 0.6 * math.exp(-0.3 * l)

    row = lambda v: v[l].reshape(1, -1)
    bf = lambda w: w[l].astype(BF16)
    w_in_b, w_out_b, w_mq_b, w_mk_b, w_mv_b, w_mo_b = map(bf, (w_in, w_out, w_mq, w_mk, w_mv, w_mo))
    w_gate_b, w_up_b, w_down_b = map(bf, (w_gate, w_up, w_down))
    lams = tuple(row(v) for v in (lam_q1, lam_k1, lam_q2, lam_k2))
    g_head = g_diff_head[l]

    tm_p = _pick(s, 512)
    tkv = _pick(s, 256)
    tq = _pick(s, 512)
    ret_c = RET_CHUNK if s % RET_CHUNK == 0 else s
    ret_c_s = RET_CHUNK if t % RET_CHUNK == 0 else t
    n_s = bd * t
    dh_m = d // H_M
    halves = dh_m // LANES
    nchunk = H_M * halves

    def mem_rows(a):
        g = a.shape[0]
        return a.reshape(g, n_mem, H_M, halves, LANES).transpose(0, 1, 3, 2, 4).reshape(g, n_mem * nchunk, LANES)

    def mem_heads(a):
        g = a.shape[0]
        return a.reshape(g, n_mem, halves, H_M, LANES).transpose(0, 1, 3, 2, 4).reshape(g, n_mem, H_M, dh_m)

    mk_p, mv_p = _memkv(mem_prompt.reshape(b * n_mem, d), row(g_mem_in), w_mk_b, w_mv_b, tm=_pick(b * n_mem, 512))
    mk_p = mk_p.reshape(b, n_mem * nchunk, LANES)
    mv_p = mv_p.reshape(b, n_mem * nchunk, LANES)
    tabs_p = _rotary_tables(jnp.arange(s), DK_R ** -0.5)
    qa, ka, va, kab, vat, qr, kr, vr, gr = _inproj(
        x_prompt.reshape(b * s, d), row(g_pre_mix), w_in_b, tabs_p, tm=tm_p, prompt=True, tkv=tkv,
        q_scale=DH_A ** -0.5 * math.log2(math.e), act_dtype=BF16)
    oa = _dattn_prompt(lams, g_head.reshape(DV_A, 1), qa, kab, vat, b=b, s=s, tq=tq, tk=tkv, lam_init=lam_init)
    r3 = lambda a: a.reshape(b, s, a.shape[-1])
    orr, ret_p = _retention(r3(qr), r3(kr), r3(vr), r3(gr), jnp.zeros((b, H_R, DK_R, DV_R), F32),
                            _retention_tables(ret_c), c=ret_c, cps=_pick(s // ret_c, 4), mm=BF16, out_dtype=BF16)
    h1, qm = _mix(oa, orr.reshape(b * s, VR_W), x_prompt.reshape(b * s, d), w_out_b, row(g_post_mix),
                  row(g_pre_mem), w_mq_b, tm=tm_p, act_dtype=BF16)
    om = _memattn(qm.reshape(b, s, d), mk_p, mv_p, tq=tm_p, nb=1, mm=BF16)
    tm_f = _pick(s, 256)
    y_p, conv_p = _ffn(h1, om.reshape(b * s, d), w_mo_b, row(g_post_mem), row(g_pre_ffn), row(g_post_ffn),
                       w_gate_b, w_up_b, conv_w[l], row(conv_b), w_down_b,
                       jnp.zeros((b, CONV_W - 1, d_ff), F32), tm=tm_f, seq=s, fw=d_ff // 2, carry=True)

    tm_s = _pick(n_s, 512)
    reps = tm_s // t
    tabs_s = tuple(jnp.tile(a, (reps, 1)) for a in _rotary_tables(past + jnp.arange(t), DK_R ** -0.5))
    qa_s, ka_s, va_s, qr_s, kr_s, vr_s, gr_s = _inproj(
        x_sample.reshape(n_s, d), row(g_pre_mix), w_in_b, tabs_s, tm=tm_s, prompt=False, tkv=tkv,
        q_scale=DH_A ** -0.5, act_dtype=F32)
    b3 = lambda a: a.reshape(bd, t, a.shape[-1])
    oa_s = _dattn_sample(page_table, lams, g_head.reshape(1, DV_A), b3(qa_s), b3(ka_s), b3(va_s),
                         cache_k[l].reshape(n_pool, PAGE_SIZE * H_A, 2 * DH_A),
                         cache_v[l].reshape(n_pool, PAGE_SIZE * H_A, DV_A),
                         pp=_pick(page_table.shape[1], 16), lam_init=lam_init)
    orr_s, ret_s = _retention(b3(qr_s), b3(kr_s), b3(vr_s), b3(gr_s), state_ret[l], _retention_tables(ret_c_s),
                              c=ret_c_s, cps=_pick(t // ret_c_s, 4), mm=F32, out_dtype=F32)
    h1_s, qm_s = _mix(oa_s.reshape(n_s, VA_W), orr_s.reshape(n_s, VR_W), x_sample.reshape(n_s, d), w_out_b,
                      row(g_post_mix), row(g_pre_mem), w_mq_b, tm=tm_s, act_dtype=F32)
    om_s = _memattn(b3(qm_s), mem_rows(cache_mem_k[l]), mem_rows(cache_mem_v[l]), tq=t, nb=_pick(bd, 4), mm=F32)
    prev = state_conv[l]
    zeros = jnp.zeros((bd, t - 2, d_ff), F32)
    p1 = jnp.concatenate([prev[:, 1:2], jnp.zeros((bd, 1, d_ff), F32), zeros], axis=1).reshape(n_s, d_ff)
    p2 = jnp.concatenate([prev, zeros], axis=1).reshape(n_s, d_ff)
    tm_fs = _pick(n_s, 256)
    y_s, a_s = _ffn(h1_s, om_s.reshape(n_s, d), w_mo_b, row(g_post_mem), row(g_pre_ffn), row(g_post_ffn),
                    w_gate_b, w_up_b, conv_w[l], row(conv_b), w_down_b, (p1, p2),
                    tm=tm_fs, seq=t, fw=d_ff // 2, carry=False)
    conv_s = a_s.reshape(bd, t, d_ff)[:, t - (CONV_W - 1):]

    return (y_p.reshape(b, s, d), y_s.reshape(bd, t, d),
            ka.reshape(1, b, s, H_A, 2 * DH_A), va.reshape(1, b, s, H_A, DV_A),
            mem_heads(mk_p)[None], mem_heads(mv_p)[None],
            ret_p[None], conv_p[None],
            ka_s.reshape(1, bd, t, H_A, 2 * DH_A), va_s.reshape(1, bd, t, H_A, DV_A),
            ret_s[None], conv_s[None])
```

```python
import functools
import math

import jax
import jax.numpy as jnp
from jax import lax
from jax.experimental import pallas as pl
from jax.experimental.pallas import tpu as pltpu

F32 = jnp.float32
BF16 = jnp.bfloat16

H_A = 4
DH_A = 64
DV_A = 2 * DH_A
H_R = 4
DK_R = 64
DV_R = 128
RET_CHUNK = 128
ROPE_BASE = 10000.0
H_M = 4
CONV_W = 3
PAGE_SIZE = 128
EPS = 1e-6
NEG = -1e30

QA_W = H_A * 2 * DH_A
KA_W = H_A * 2 * DH_A
VA_W = H_A * DV_A
QR_W = H_R * DK_R
KR_W = H_R * DK_R
VR_W = H_R * DV_R
GR_W = H_R * DV_R

VMEM_LIMIT = 56 * 1024 * 1024
SUBLANES = 8
LANES = 128

_NT = (((1,), (1,)), ((), ()))
_TN = (((0,), (0,)), ((), ()))


def _params(*sem):
    return pltpu.CompilerParams(dimension_semantics=sem, vmem_limit_bytes=VMEM_LIMIT)


def _const_spec(shape):
    nd = len(shape)
    return pl.BlockSpec(shape, lambda *_: (0,) * nd, pipeline_mode=pl.Buffered(1))


def _rms(x, g=None):
    y = x * lax.rsqrt(jnp.mean(x * x, axis=-1, keepdims=True) + EPS)
    return y if g is None else y * g


def _gelu_exact(x):
    return 0.5 * x * (1.0 + lax.erf(x * (2.0 ** -0.5)))


def _diff_lambda(lq1, lk1, lq2, lk2, lam_init):
    a = jnp.sum(lq1[...] * lk1[...], axis=-1, keepdims=True)
    b = jnp.sum(lq2[...] * lk2[...], axis=-1, keepdims=True)
    return jnp.exp(a) - jnp.exp(b) + lam_init


def _rotate(x, cos_t, sin_t):
    half = DK_R // 2
    w = x.shape[-1]
    lane = lax.broadcasted_iota(jnp.int32, x.shape, 1)
    swapped = jnp.where(lane % DK_R < half, pltpu.roll(x, w - half, 1), pltpu.roll(x, half, 1))
    return x * cos_t + swapped * sin_t


def _inproj_kernel(x_ref, g_ref, w_ref, cq_ref, sq_ref, ck_ref, sk_ref, *out_refs, prompt, tkv, q_scale):
    if prompt:
        qa_ref, ka_ref, va_ref, kab_ref, vat_ref, qr_ref, kr_ref, vr_ref, gr_ref = out_refs
    else:
        qa_ref, ka_ref, va_ref, qr_ref, kr_ref, vr_ref, gr_ref = out_refs
    xn = _rms(x_ref[...], g_ref[...]).astype(BF16)
    tm = xn.shape[0]

    def proj(c0, width):
        return jnp.dot(xn, w_ref[:, c0:c0 + width], preferred_element_type=F32)

    def store_heads(ref, val, width):
        if prompt:
            for h in range(H_A):
                ref[pl.ds(h, tm, stride=H_A), :] = val[:, h * width:(h + 1) * width]
        else:
            ref[...] = val

    c = 0
    qa = proj(c, QA_W) * q_scale
    if prompt:
        tqb = qa_ref.shape[-1]
        for r in range(tm // tqb):
            qa_ref[r] = qa[r * tqb:(r + 1) * tqb, :].T.astype(qa_ref.dtype)
    else:
        qa_ref[...] = qa.astype(qa_ref.dtype)
    c += QA_W
    ka = proj(c, KA_W)
    store_heads(ka_ref, ka, 2 * DH_A)
    c += KA_W
    va = proj(c, VA_W)
    store_heads(va_ref, va, DV_A)
    c += VA_W
    if prompt:
        kab_ref[...] = ka.astype(BF16)
        for r in range(tm // tkv):
            vat_ref[r] = va[r * tkv:(r + 1) * tkv, :].T.astype(BF16)
    qr_ref[...] = _rotate(proj(c, QR_W), cq_ref[...], sq_ref[...]).astype(qr_ref.dtype)
    c += QR_W
    kr_ref[...] = _rotate(proj(c, KR_W), ck_ref[...], sk_ref[...]).astype(kr_ref.dtype)
    c += KR_W
    vr_ref[...] = proj(c, VR_W).astype(vr_ref.dtype)
    c += VR_W
    gr_ref[...] = proj(c, GR_W)


def _inproj(x2d, g, w_in, tabs, *, tm, prompt, tkv, tqb, q_scale, act_dtype):
    n, d = x2d.shape
    period = tabs[0].shape[0]
    npb = period // tm
    row = lambda i: (i, 0)
    tab = lambda i: (i % npb, 0)
    in_specs = [pl.BlockSpec((tm, d), row), _const_spec((1, d)), _const_spec(w_in.shape)]
    in_specs += [pl.BlockSpec((tm, QR_W), tab)] * 4
    if prompt:
        out_shape = [jax.ShapeDtypeStruct((n // tqb, QA_W, tqb), act_dtype)]
        out_specs = [pl.BlockSpec((tm // tqb, QA_W, tqb), lambda i: (i, 0, 0))]
        out_shape += [jax.ShapeDtypeStruct((n * H_A, 2 * DH_A), F32),
                      jax.ShapeDtypeStruct((n * H_A, DV_A), F32),
                      jax.ShapeDtypeStruct((n, KA_W), BF16),
                      jax.ShapeDtypeStruct((n // tkv, VA_W, tkv), BF16)]
        out_specs += [pl.BlockSpec((tm * H_A, 2 * DH_A), row), pl.BlockSpec((tm * H_A, DV_A), row),
                      pl.BlockSpec((tm, KA_W), row),
                      pl.BlockSpec((tm // tkv, VA_W, tkv), lambda i: (i, 0, 0))]
    else:
        out_shape = [jax.ShapeDtypeStruct((n, QA_W), act_dtype),
                     jax.ShapeDtypeStruct((n, KA_W), F32), jax.ShapeDtypeStruct((n, VA_W), F32)]
        out_specs = [pl.BlockSpec((tm, QA_W), row), pl.BlockSpec((tm, KA_W), row), pl.BlockSpec((tm, VA_W), row)]
    out_shape += [jax.ShapeDtypeStruct((n, QR_W), act_dtype),
                  jax.ShapeDtypeStruct((n, KR_W), act_dtype),
                  jax.ShapeDtypeStruct((n, VR_W), act_dtype),
                  jax.ShapeDtypeStruct((n, GR_W), F32)]
    out_specs += [pl.BlockSpec((tm, QR_W), row), pl.BlockSpec((tm, KR_W), row),
                  pl.BlockSpec((tm, VR_W), row), pl.BlockSpec((tm, GR_W), row)]
    return pl.pallas_call(
        functools.partial(_inproj_kernel, prompt=prompt, tkv=tkv, q_scale=q_scale),
        grid=(n // tm,), in_specs=in_specs, out_specs=out_specs, out_shape=out_shape,
        compiler_params=_params("parallel"), name="inproj",
    )(x2d, g, w_in, *tabs)


def _dattn_p_kernel(lq1, lk1, lq2, lk2, g_ref, qt_ref, k_ref, vt_ref, o_ref, qbd_scr, st_scr, p_scr, acc_scr,
                    *, tq, tk, lam_init):
    qi = pl.program_id(2)
    nfull = 2 * qi
    qt = qt_ref[0]
    row = lax.broadcasted_iota(jnp.int32, qt.shape, 0)
    zero = jnp.zeros_like(qt)
    qbd_scr[:, 0:tq] = jnp.where(row < DH_A, qt, zero)
    qbd_scr[:, tq:2 * tq] = jnp.where(row >= DH_A, qt, zero)
    acc_scr[...] = jnp.zeros(acc_scr.shape, F32)
    p_scr[1] = jnp.zeros(p_scr.shape[1:], BF16)

    gw = 2 * LANES
    ngroup = 2 * tq // gw

    def tile_class(d, c):
        q0 = (c * LANES) % tq
        if q0 + LANES - 1 < d * tk:
            return "skip"
        if q0 >= d * tk + tk - 1:
            return "full"
        return "mask"

    def group_live(d, g):
        return any(tile_class(d, c) != "skip" for c in range(g * gw // LANES, (g + 1) * gw // LANES))

    def scores(j, slot, g):
        kj = k_ref[pl.ds(pl.multiple_of(j * tk, tk), tk), :]
        st_scr[slot, :, g * gw:(g + 1) * gw] = jnp.dot(
            kj, qbd_scr[:, g * gw:(g + 1) * gw], preferred_element_type=F32)

    def block(j, slot, m, l, diag):
        vt_prev = vt_ref[jnp.maximum(j - 1, 0)]
        m_out, l_out = [], []
        for g in range(ngroup):
            gs = slice(g * gw, (g + 1) * gw)
            if diag is None or (diag == 0 and group_live(1, g)):
                scores(j + 1, 1 - slot, g)
            acc = acc_scr[:, gs]
            if diag != 1 or group_live(0, g):
                acc = acc + jnp.dot(vt_prev, p_scr[1 - slot, :, gs], preferred_element_type=F32)
            alphas = []
            for c in range(g * gw // LANES, (g + 1) * gw // LANES):
                cs = slice(c * LANES, (c + 1) * LANES)
                cls = "full" if diag is None else tile_class(diag, c)
                if cls == "skip":
                    m_out.append(m[:, cs])
                    l_out.append(l[:, cs])
                    continue
                st = st_scr[slot, :, cs]
                if cls == "mask":
                    kpos = diag * tk + lax.broadcasted_iota(jnp.int32, st.shape, 0)
                    qpos = (c * LANES) % tq + lax.broadcasted_iota(jnp.int32, st.shape, 1)
                    st = jnp.where(kpos <= qpos, st, NEG)
                m_new = jnp.maximum(m[:, cs], jnp.max(st, axis=0, keepdims=True))
                alpha = jnp.exp2(m[:, cs] - m_new)
                p = jnp.exp2(st - m_new)
                l_out.append(alpha * l[:, cs] + jnp.sum(p, axis=0, keepdims=True))
                m_out.append(m_new)
                alphas.append(alpha)
                p_scr[slot, :, cs] = p.astype(BF16)
            assert len(alphas) in (0, gw // LANES), "a lane group is skipped or processed as a whole"
            acc_scr[:, gs] = acc * jnp.concatenate(alphas, axis=-1) if alphas else acc
        return jnp.concatenate(m_out, axis=-1), jnp.concatenate(l_out, axis=-1)

    def pair(i, carry):
        m, l = block(2 * i, 0, carry[0], carry[1], None)
        return block(2 * i + 1, 1, m, l, None)

    for g in range(ngroup):
        scores(0, 0, g)
    carry = (jnp.full((1, 2 * tq), NEG, F32), jnp.zeros((1, 2 * tq), F32))
    carry = lax.fori_loop(0, qi, pair, carry)
    carry = block(nfull, 0, carry[0], carry[1], 0)
    carry = block(nfull + 1, 1, carry[0], carry[1], 1)
    vt_last = vt_ref[nfull + 1]
    for g in range(ngroup):
        if group_live(1, g):
            gs = slice(g * gw, (g + 1) * gw)
            acc_scr[:, gs] = acc_scr[:, gs] + jnp.dot(vt_last, p_scr[1, :, gs], preferred_element_type=F32)
    lam = _diff_lambda(lq1, lk1, lq2, lk2, lam_init)
    o = acc_scr[...] * (1.0 / carry[1])
    o = o[:, :tq] - lam * o[:, tq:]
    y = o * lax.rsqrt(jnp.mean(o * o, axis=0, keepdims=True) + EPS) * g_ref[...] * (1.0 - lam_init)
    o_ref[...] = y.T.astype(o_ref.dtype)


def _dattn_prompt(lams, g_col, qat, kab, vat, *, b, s, tq, tk, lam_init):
    n = kab.shape[0]
    hw = 2 * DH_A
    nq = s // tq
    assert tq == 2 * tk and s % tq == 0, "the kernel walks key blocks in pairs"
    assert qat.shape == (n // tq, QA_W, tq)
    lam_specs = [_const_spec((1, DH_A))] * 4
    return pl.pallas_call(
        functools.partial(_dattn_p_kernel, tq=tq, tk=tk, lam_init=lam_init),
        grid=(b, H_A, nq),
        in_specs=lam_specs + [
            _const_spec((DV_A, 1)),
            pl.BlockSpec((1, hw, tq), lambda bi, h, qi: (bi * nq + qi, h, 0)),
            pl.BlockSpec((s, hw), lambda bi, h, qi: (bi, h)),
            pl.BlockSpec((s // tk, DV_A, tk), lambda bi, h, qi: (bi, h, 0)),
        ],
        out_specs=pl.BlockSpec((tq, DV_A), lambda bi, h, qi: (bi * nq + qi, h)),
        out_shape=jax.ShapeDtypeStruct((n, VA_W), BF16),
        scratch_shapes=[pltpu.VMEM((hw, 2 * tq), BF16), pltpu.VMEM((2, tk, 2 * tq), F32),
                        pltpu.VMEM((2, tk, 2 * tq), BF16), pltpu.VMEM((DV_A, 2 * tq), F32)],
        compiler_params=_params("parallel", "parallel", "arbitrary"), name="dattn_prompt",
    )(*lams, g_col, qat, kab, vat)


def _dattn_s_kernel(pt_ref, lq1, lk1, lq2, lk2, g_ref, q_ref, kn_ref, vn_ref, ck_hbm, cv_hbm, o_ref,
                    kbuf, vbuf, sems, m_scr, l_scr, acc_scr, *, pp, t, lam_init):
    bi = pl.program_id(0)
    gi = pl.program_id(1)
    ng = pl.num_programs(1)
    nsteps = pl.num_programs(0) * ng
    step = bi * ng + gi
    slot = step % 2
    rows = PAGE_SIZE * H_A
    hm = 2 * t
    nkeys = pp * PAGE_SIZE

    def copies(step_, slot_):
        b_ = step_ // ng
        g_ = step_ % ng
        out = []
        for p in range(pp):
            page = pt_ref[b_, g_ * pp + p]
            out.append(pltpu.make_async_copy(ck_hbm.at[page], kbuf.at[slot_, pl.ds(p * rows, rows)], sems.at[0, slot_]))
            out.append(pltpu.make_async_copy(cv_hbm.at[page], vbuf.at[slot_, pl.ds(p * rows, rows)], sems.at[1, slot_]))
        return out

    @pl.when(step == 0)
    def _():
        for c in copies(step, slot):
            c.start()

    @pl.when(step + 1 < nsteps)
    def _():
        for c in copies(step + 1, 1 - slot):
            c.start()

    @pl.when(gi == 0)
    def _():
        m_scr[...] = jnp.full(m_scr.shape, NEG, F32)
        l_scr[...] = jnp.zeros(l_scr.shape, F32)
        acc_scr[...] = jnp.zeros(acc_scr.shape, F32)

    q = q_ref[0]
    qbd = []
    for h in range(H_A):
        qh = q[:, h * 2 * DH_A:(h + 1) * 2 * DH_A]
        lane = lax.broadcasted_iota(jnp.int32, qh.shape, 1)
        qbd.append(jnp.concatenate([jnp.where(lane < DH_A, qh, 0.0), jnp.where(lane >= DH_A, qh, 0.0)], axis=0))

    def update(s_all, v_heads):
        m = m_scr[...]
        m_new = jnp.maximum(m, jnp.max(s_all, axis=-1, keepdims=True))
        alpha = jnp.exp(m - m_new)
        p = jnp.exp(s_all - m_new)
        l_scr[...] = alpha * l_scr[...] + jnp.sum(p, axis=-1, keepdims=True)
        m_scr[...] = m_new
        pv = [jnp.dot(p[h * hm:(h + 1) * hm].astype(v_heads[h].dtype), v_heads[h], preferred_element_type=F32)
              for h in range(H_A)]
        acc_scr[...] = acc_scr[...] * alpha + jnp.concatenate(pv, axis=0)

    for c in copies(step, slot):
        c.wait()

    s_heads, v_heads = [], []
    for h in range(H_A):
        kh = kbuf[slot, pl.ds(h, nkeys, stride=H_A), :].astype(BF16)
        v_heads.append(vbuf[slot, pl.ds(h, nkeys, stride=H_A), :].astype(BF16))
        s_heads.append(lax.dot_general(qbd[h].astype(BF16), kh, _NT, preferred_element_type=F32))
    update(jnp.concatenate(s_heads, axis=0), v_heads)

    @pl.when(gi == ng - 1)
    def _():
        kn = kn_ref[0]
        vn = vn_ref[0]
        s_heads, v_heads = [], []
        for h in range(H_A):
            s_heads.append(lax.dot_general(qbd[h], kn[:, h * 2 * DH_A:(h + 1) * 2 * DH_A], _NT,
                                           preferred_element_type=F32))
            v_heads.append(vn[:, h * DV_A:(h + 1) * DV_A])
        s_new = jnp.concatenate(s_heads, axis=0)
        kidx = lax.broadcasted_iota(jnp.int32, s_new.shape, 1)
        qidx = lax.broadcasted_iota(jnp.int32, s_new.shape, 0) % t
        update(jnp.where(kidx <= qidx, s_new, NEG), v_heads)

        lam = _diff_lambda(lq1, lk1, lq2, lk2, lam_init)
        o = acc_scr[...] * (1.0 / l_scr[...])
        for h in range(H_A):
            oh = o[h * hm:h * hm + t] - lam * o[h * hm + t:(h + 1) * hm]
            y = _rms(oh, g_ref[...]) * (1.0 - lam_init)
            o_ref[0, :, h * DV_A:(h + 1) * DV_A] = y.astype(o_ref.dtype)


def _dattn_sample(page_table, lams, g_row, qa, kn, vn, ck, cv, *, pp, lam_init):
    bd, t, _ = qa.shape
    npages = page_table.shape[1]
    rows = PAGE_SIZE * H_A
    blk = lambda bi, gi, pt: (bi, 0, 0)
    cst = lambda bi, gi, pt: (0, 0)
    grid_spec = pltpu.PrefetchScalarGridSpec(
        num_scalar_prefetch=1,
        grid=(bd, npages // pp),
        in_specs=[pl.BlockSpec((1, DH_A), cst)] * 4 + [
            pl.BlockSpec((1, DV_A), cst),
            pl.BlockSpec((1, t, QA_W), blk),
            pl.BlockSpec((1, t, KA_W), blk),
            pl.BlockSpec((1, t, VA_W), blk),
            pl.BlockSpec(memory_space=pl.ANY),
            pl.BlockSpec(memory_space=pl.ANY),
        ],
        out_specs=pl.BlockSpec((1, t, VA_W), blk),
        scratch_shapes=[
            pltpu.VMEM((2, pp * rows, 2 * DH_A), F32),
            pltpu.VMEM((2, pp * rows, DV_A), F32),
            pltpu.SemaphoreType.DMA((2, 2)),
            pltpu.VMEM((H_A * 2 * t, 1), F32),
            pltpu.VMEM((H_A * 2 * t, 1), F32),
            pltpu.VMEM((H_A * 2 * t, DV_A), F32),
        ],
    )
    return pl.pallas_call(
        functools.partial(_dattn_s_kernel, pp=pp, t=t, lam_init=lam_init),
        grid_spec=grid_spec,
        out_shape=jax.ShapeDtypeStruct((bd, t, VA_W), F32),
        compiler_params=_params("arbitrary", "arbitrary"), name="dattn_sample",
    )(page_table, *lams, g_row, qa, kn, vn, ck, cv)


def _ret_kernel(q_ref, k_ref, v_ref, g_ref, s0_ref, dmat_ref, qdec_ref, kdec_ref, sdec_ref,
                o_ref, sout_ref, s_scr, *, c, cps, mm):
    ci = pl.program_id(1)

    @pl.when(ci == 0)
    def _():
        s_scr[...] = s0_ref[0].reshape(s_scr.shape)

    head_of_lane = lax.broadcasted_iota(jnp.int32, (c, QR_W), 1) // DK_R
    for step in range(cps):
        r0 = step * c
        q = q_ref[0, r0:r0 + c, :]
        k = k_ref[0, r0:r0 + c, :]
        v = v_ref[0, r0:r0 + c, :].astype(mm)
        gate = g_ref[0, r0:r0 + c, :]
        s_old = s_scr[...]
        s_mm = s_old.astype(mm)
        k_mm = k.astype(mm)
        kd = (k.astype(F32) * kdec_ref[...]).astype(mm)
        upd = lax.dot_general(kd, v, _TN, preferred_element_type=F32)
        for h in range(H_R):
            qh = jnp.where(head_of_lane == h, q, jnp.zeros_like(q)).astype(mm)
            vh = v[:, h * DV_R:(h + 1) * DV_R]
            s = lax.dot_general(qh, k_mm, _NT, preferred_element_type=F32) * dmat_ref[h]
            intra = jnp.dot(s.astype(mm), vh, preferred_element_type=F32)
            cross = jnp.dot(qh, s_mm, preferred_element_type=F32) * qdec_ref[:, h * DV_R:(h + 1) * DV_R]
            gh = gate[:, h * DV_R:(h + 1) * DV_R]
            y = _rms(intra + cross) * (gh * jax.nn.sigmoid(gh))
            o_ref[0, r0:r0 + c, h * DV_R:(h + 1) * DV_R] = y.astype(o_ref.dtype)
        diag = [upd[h * DK_R:(h + 1) * DK_R, h * DV_R:(h + 1) * DV_R] for h in range(H_R)]
        s_scr[...] = s_old * sdec_ref[...] + jnp.concatenate(diag, axis=0)

    @pl.when(ci == pl.num_programs(1) - 1)
    def _():
        sout_ref[0] = s_scr[...].reshape(sout_ref.shape[1:])


def _retention(qr, kr, vr, gr, s0, tabs, *, c, cps, mm, out_dtype):
    b, t, _ = qr.shape
    dmat, qdec, kdec, sdec = tabs
    tt = c * cps
    blk = lambda bi, ci: (bi, ci, 0)
    st = lambda bi, ci: (bi, 0, 0, 0)
    return pl.pallas_call(
        functools.partial(_ret_kernel, c=c, cps=cps, mm=mm),
        grid=(b, t // tt),
        in_specs=[pl.BlockSpec((1, tt, QR_W), blk), pl.BlockSpec((1, tt, KR_W), blk),
                  pl.BlockSpec((1, tt, VR_W), blk), pl.BlockSpec((1, tt, GR_W), blk),
                  pl.BlockSpec((1, H_R, DK_R, DV_R), st),
                  _const_spec(dmat.shape), _const_spec(qdec.shape), _const_spec(kdec.shape),
                  _const_spec(sdec.shape)],
        out_specs=[pl.BlockSpec((1, tt, VR_W), blk), pl.BlockSpec((1, H_R, DK_R, DV_R), st)],
        out_shape=[jax.ShapeDtypeStruct((b, t, VR_W), out_dtype),
                   jax.ShapeDtypeStruct((b, H_R, DK_R, DV_R), F32)],
        scratch_shapes=[pltpu.VMEM((H_R * DK_R, DV_R), F32)],
        compiler_params=_params("parallel", "arbitrary"), name="retention",
    )(qr, kr, vr, gr, s0, dmat, qdec, kdec, sdec)


def _mix_kernel(oa_ref, or_ref, h_ref, wo_ref, gpost_ref, gpre_ref, wq_ref, h1_ref, qm_ref, *, dh_m):
    mix_in = jnp.concatenate([oa_ref[...], or_ref[...]], axis=-1).astype(BF16)
    mix = jnp.dot(mix_in, wo_ref[...], preferred_element_type=F32)
    h1 = h_ref[...] + _rms(mix, gpost_ref[...])
    h1_ref[...] = h1
    xn = _rms(h1, gpre_ref[...]).astype(BF16)
    qm_ref[...] = (jnp.dot(xn, wq_ref[...], preferred_element_type=F32) * dh_m ** -0.5).astype(qm_ref.dtype)


def _mix(oa, orr, h, w_out, g_post, g_pre, w_mq, *, tm, act_dtype):
    n, d = h.shape
    row = lambda i: (i, 0)
    return pl.pallas_call(
        functools.partial(_mix_kernel, dh_m=d // H_M),
        grid=(n // tm,),
        in_specs=[pl.BlockSpec((tm, VA_W), row), pl.BlockSpec((tm, VR_W), row), pl.BlockSpec((tm, d), row),
                  _const_spec(w_out.shape), _const_spec((1, d)), _const_spec((1, d)), _const_spec(w_mq.shape)],
        out_specs=[pl.BlockSpec((tm, d), row), pl.BlockSpec((tm, d), row)],
        out_shape=[jax.ShapeDtypeStruct((n, d), F32), jax.ShapeDtypeStruct((n, d), act_dtype)],
        compiler_params=_params("parallel"), name="mix_out",
    )(oa, orr, h, w_out, g_post, g_pre, w_mq)


def _mem_chunk(h, c):
    return c * H_M + h


def _memkv_kernel(mem_ref, g_ref, wk_ref, wv_ref, mk_ref, mv_ref):
    mn = _rms(mem_ref[...], g_ref[...]).astype(BF16)
    tm, d = mn.shape
    halves = d // H_M // LANES
    nchunk = H_M * halves
    for w_ref, o_ref in ((wk_ref, mk_ref), (wv_ref, mv_ref)):
        y = jnp.dot(mn, w_ref[...], preferred_element_type=F32)
        for h in range(H_M):
            for c in range(halves):
                col = (h * halves + c) * LANES
                o_ref[pl.ds(_mem_chunk(h, c), tm, stride=nchunk), :] = y[:, col:col + LANES]


def _memkv(mem2d, g, w_mk, w_mv, *, tm):
    n, d = mem2d.shape
    nchunk = d // LANES
    row = lambda i: (i, 0)
    return pl.pallas_call(
        _memkv_kernel, grid=(n // tm,),
        in_specs=[pl.BlockSpec((tm, d), row), _const_spec((1, d)), _const_spec(w_mk.shape), _const_spec(w_mv.shape)],
        out_specs=[pl.BlockSpec((tm * nchunk, LANES), row)] * 2,
        out_shape=[jax.ShapeDtypeStruct((n * nchunk, LANES), F32)] * 2,
        compiler_params=_params("parallel"), name="mem_kv",
    )(mem2d, g, w_mk, w_mv)


def _memattn_kernel(q_ref, mk_ref, mv_ref, o_ref, *, mm, nb):
    d = q_ref.shape[-1]
    dh = d // H_M
    halves = dh // LANES
    nchunk = H_M * halves
    n_mem = mk_ref.shape[1] // nchunk

    def head(ref, bi, h):
        parts = [ref[bi, pl.ds(_mem_chunk(h, c), n_mem, stride=nchunk), :] for c in range(halves)]
        return jnp.concatenate(parts, axis=-1).astype(mm)

    for bi in range(nb):
        q = q_ref[bi]
        for h in range(H_M):
            qh = q[:, h * dh:(h + 1) * dh].astype(mm)
            s = lax.dot_general(qh, head(mk_ref, bi, h), _NT, preferred_element_type=F32)
            p = jnp.exp(s - jnp.max(s, axis=-1, keepdims=True))
            p = p * (1.0 / jnp.sum(p, axis=-1, keepdims=True))
            o = jnp.dot(p.astype(mm), head(mv_ref, bi, h), preferred_element_type=F32)
            o_ref[bi, :, h * dh:(h + 1) * dh] = o.astype(o_ref.dtype)


def _memattn(qm, mk, mv, *, tq, nb, mm):
    b, t, d = qm.shape
    rows = mk.shape[1]
    return pl.pallas_call(
        functools.partial(_memattn_kernel, mm=mm, nb=nb),
        grid=(b // nb, t // tq),
        in_specs=[pl.BlockSpec((nb, tq, d), lambda bi, qi: (bi, qi, 0)),
                  pl.BlockSpec((nb, rows, LANES), lambda bi, qi: (bi, 0, 0)),
                  pl.BlockSpec((nb, rows, LANES), lambda bi, qi: (bi, 0, 0))],
        out_specs=pl.BlockSpec((nb, tq, d), lambda bi, qi: (bi, qi, 0)),
        out_shape=jax.ShapeDtypeStruct((b, t, d), qm.dtype),
        compiler_params=_params("parallel", "arbitrary"), name="mem_attn",
    )(qm, mk, mv)


def _ffn_kernel(h1_ref, o_ref, wmo_ref, gpm_ref, gpre_ref, gpost_ref, wg_ref, wu_ref, cw_ref, cb_ref, wd_ref,
                *rest, tm, seq, fw, carry):
    if carry:
        prev_ref, y_ref, cnew_ref, a_scr, h_scr = rest
    else:
        p1_ref, p2_ref, y_ref, a_ref, a_scr, h_scr = rest
    i = pl.program_id(0)
    pad = SUBLANES
    m = jnp.dot(o_ref[...].astype(BF16), wmo_ref[...], preferred_element_type=F32)
    h2 = h1_ref[...] + _rms(m, gpm_ref[...])
    xn = _rms(h2, gpre_ref[...]).astype(BF16)
    d_ff = wg_ref.shape[1]

    if carry:
        @pl.when(i % (seq // tm) == 0)
        def _():
            a_scr[pad - 2:pad, :] = prev_ref[0]
    else:
        a_scr[0:pad, :] = jnp.zeros((pad, d_ff), F32)
        tpos = lax.broadcasted_iota(jnp.int32, (tm, fw), 0) % seq

    for c0 in range(0, d_ff, fw):
        cs = slice(c0, c0 + fw)
        a = jnp.dot(xn, wg_ref[:, cs], preferred_element_type=F32)
        a_scr[pad:pad + tm, cs] = a
        s1 = a_scr[pad - 1:pad - 1 + tm, cs]
        s2 = a_scr[pad - 2:pad - 2 + tm, cs]
        if not carry:
            s1 = jnp.where(tpos == 0, p1_ref[:, cs], s1)
            s2 = jnp.where(tpos < 2, p2_ref[:, cs], s2)
        conv = cb_ref[:, cs] + s2 * cw_ref[0:1, cs] + s1 * cw_ref[1:2, cs] + a * cw_ref[2:3, cs]
        up = jnp.dot(xn, wu_ref[:, cs], preferred_element_type=F32)
        h_scr[:, cs] = (_gelu_exact(conv) * up).astype(BF16)

    if carry:
        cnew_ref[0] = a_scr[pad + tm - 2:pad + tm, :]
        a_scr[0:pad, :] = a_scr[tm:tm + pad, :]
    else:
        a_ref[...] = a_scr[pad:pad + tm, :]
    f = jnp.dot(h_scr[...], wd_ref[...], preferred_element_type=F32)
    y_ref[...] = h2 + _rms(f, gpost_ref[...])


def _ffn(h1, o, w_mo, g_post_mem, g_pre, g_post, w_gate, w_up, conv_w, conv_b, w_down, prev, *, tm, seq, fw, carry):
    n, d = h1.shape
    d_ff = w_gate.shape[1]
    row = lambda i: (i, 0)
    in_specs = [pl.BlockSpec((tm, d), row), pl.BlockSpec((tm, d), row), _const_spec(w_mo.shape),
                _const_spec((1, d)), _const_spec((1, d)), _const_spec((1, d)),
                _const_spec(w_gate.shape), _const_spec(w_up.shape), _const_spec(conv_w.shape),
                _const_spec((1, d_ff)), _const_spec(w_down.shape)]
    out_specs = [pl.BlockSpec((tm, d), row)]
    out_shape = [jax.ShapeDtypeStruct((n, d), F32)]
    if carry:
        tps = seq // tm
        in_specs += [pl.BlockSpec((1, CONV_W - 1, d_ff), lambda i: (i // tps, 0, 0))]
        out_specs += [pl.BlockSpec((1, CONV_W - 1, d_ff), lambda i: (i // tps, 0, 0))]
        out_shape += [jax.ShapeDtypeStruct((n // seq, CONV_W - 1, d_ff), F32)]
        extra = (prev,)
    else:
        in_specs += [pl.BlockSpec((tm, d_ff), row)] * 2
        out_specs += [pl.BlockSpec((tm, d_ff), row)]
        out_shape += [jax.ShapeDtypeStruct((n, d_ff), F32)]
        extra = prev
    return pl.pallas_call(
        functools.partial(_ffn_kernel, tm=tm, seq=seq, fw=fw, carry=carry),
        grid=(n // tm,), in_specs=in_specs, out_specs=out_specs, out_shape=out_shape,
        scratch_shapes=[pltpu.VMEM((tm + SUBLANES, d_ff), F32), pltpu.VMEM((tm, d_ff), BF16)],
        compiler_params=_params("arbitrary"), name="mem_out_ffn",
    )(h1, o, w_mo, g_post_mem, g_pre, g_post, w_gate, w_up, conv_w, conv_b, w_down, *extra)


def _rotary_tables(pos, k_scale):
    half = DK_R // 2
    inv = 1.0 / (ROPE_BASE ** jnp.linspace(0.0, 1.0, half, dtype=F32))
    ang = pos.astype(F32)[:, None] * inv[None, :]
    cos, sin = jnp.cos(ang), jnp.sin(ang)
    cos_t = jnp.tile(jnp.concatenate([cos, cos], axis=-1), (1, H_R))
    sin_t = jnp.tile(jnp.concatenate([-sin, sin], axis=-1), (1, H_R))
    return cos_t, sin_t, cos_t * k_scale, sin_t * k_scale


def _retention_tables(c):
    log_g = jnp.log(1.0 - 2.0 ** (-5.0 - jnp.arange(H_R, dtype=F32)))
    i = jnp.arange(c, dtype=F32)
    diff = i[:, None] - i[None, :]
    dmat = jnp.where(diff[None] >= 0, jnp.exp(jnp.maximum(diff, 0.0)[None] * log_g[:, None, None]), 0.0)
    qdec = jnp.exp((i + 1.0)[:, None] * log_g[None, :])
    kdec = jnp.exp((c - 1.0 - i)[:, None] * log_g[None, :])
    sdec = jnp.exp(c * log_g)
    return (dmat, jnp.repeat(qdec, DV_R, axis=1), jnp.repeat(kdec, DK_R, axis=1),
            jnp.broadcast_to(jnp.repeat(sdec, DK_R)[:, None], (H_R * DK_R, DV_R)))


def _pick(n, pref):
    t = min(n, pref)
    while n % t:
        t //= 2
    return t


def kernel(x_prompt, x_sample, cache_k, cache_v, cache_mem_k, cache_mem_v, state_ret, state_conv, page_table, mem_prompt, w_in, w_out, lam_q1, lam_k1, lam_q2, lam_k2, g_diff_head, g_pre_mix, g_post_mix, g_pre_mem, g_post_mem, g_mem_in, w_mq, w_mk, w_mv, w_mo, g_pre_ffn, g_post_ffn, w_gate, w_up, conv_w, conv_b, w_down):
    b, s, d = x_prompt.shape
    bd, t, _ = x_sample.shape
    depth = w_in.shape[0]
    n_mem = mem_prompt.shape[1]
    d_ff = w_gate.shape[-1]
    past = page_table.shape[1] * PAGE_SIZE
    n_pool = cache_k.shape[1]
    assert depth == 1, "the sample path keeps the layer loop unrolled for a single layer"
    l = 0
    lam_init = 0.8 - 0.6 * math.exp(-0.3 * l)

    row = lambda v: v[l].reshape(1, -1)
    bf = lambda w: w[l].astype(BF16)
    w_in_b, w_out_b, w_mq_b, w_mk_b, w_mv_b, w_mo_b = map(bf, (w_in, w_out, w_mq, w_mk, w_mv, w_mo))
    w_gate_b, w_up_b, w_down_b = map(bf, (w_gate, w_up, w_down))
    lams = tuple(row(v) for v in (lam_q1, lam_k1, lam_q2, lam_k2))
    g_head = g_diff_head[l]

    tm_p = _pick(s, 512)
    tkv = _pick(s, 256)
    tq = _pick(s, 512)
    ret_c = RET_CHUNK if s % RET_CHUNK == 0 else s
    ret_c_s = RET_CHUNK if t % RET_CHUNK == 0 else t
    n_s = bd * t
    dh_m = d // H_M
    halves = dh_m // LANES
    nchunk = H_M * halves

    def mem_rows(a):
        g = a.shape[0]
        return a.reshape(g, n_mem, H_M, halves, LANES).transpose(0, 1, 3, 2, 4).reshape(g, n_mem * nchunk, LANES)

    def mem_heads(a):
        g = a.shape[0]
        return a.reshape(g, n_mem, halves, H_M, LANES).transpose(0, 1, 3, 2, 4).reshape(g, n_mem, H_M, dh_m)

    mk_p, mv_p = _memkv(mem_prompt.reshape(b * n_mem, d), row(g_mem_in), w_mk_b, w_mv_b, tm=_pick(b * n_mem, 512))
    mk_p = mk_p.reshape(b, n_mem * nchunk, LANES)
    mv_p = mv_p.reshape(b, n_mem * nchunk, LANES)
    tabs_p = _rotary_tables(jnp.arange(s), DK_R ** -0.5)
    qat, ka, va, kab, vat, qr, kr, vr, gr = _inproj(
        x_prompt.reshape(b * s, d), row(g_pre_mix), w_in_b, tabs_p, tm=tm_p, prompt=True, tkv=tkv, tqb=tq,
        q_scale=DH_A ** -0.5 * math.log2(math.e), act_dtype=BF16)
    oa = _dattn_prompt(lams, g_head.reshape(DV_A, 1), qat, kab, vat, b=b, s=s, tq=tq, tk=tkv, lam_init=lam_init)
    r3 = lambda a: a.reshape(b, s, a.shape[-1])
    orr, ret_p = _retention(r3(qr), r3(kr), r3(vr), r3(gr), jnp.zeros((b, H_R, DK_R, DV_R), F32),
                            _retention_tables(ret_c), c=ret_c, cps=_pick(s // ret_c, 4), mm=BF16, out_dtype=BF16)
    h1, qm = _mix(oa, orr.reshape(b * s, VR_W), x_prompt.reshape(b * s, d), w_out_b, row(g_post_mix),
                  row(g_pre_mem), w_mq_b, tm=tm_p, act_dtype=BF16)
    om = _memattn(qm.reshape(b, s, d), mk_p, mv_p, tq=tm_p, nb=1, mm=BF16)
    tm_f = _pick(s, 256)
    y_p, conv_p = _ffn(h1, om.reshape(b * s, d), w_mo_b, row(g_post_mem), row(g_pre_ffn), row(g_post_ffn),
                       w_gate_b, w_up_b, conv_w[l], row(conv_b), w_down_b,
                       jnp.zeros((b, CONV_W - 1, d_ff), F32), tm=tm_f, seq=s, fw=d_ff // 2, carry=True)

    tm_s = _pick(n_s, 512)
    reps = tm_s // t
    tabs_s = tuple(jnp.tile(a, (reps, 1)) for a in _rotary_tables(past + jnp.arange(t), DK_R ** -0.5))
    qa_s, ka_s, va_s, qr_s, kr_s, vr_s, gr_s = _inproj(
        x_sample.reshape(n_s, d), row(g_pre_mix), w_in_b, tabs_s, tm=tm_s, prompt=False, tkv=tkv, tqb=tq,
        q_scale=DH_A ** -0.5, act_dtype=F32)
    b3 = lambda a: a.reshape(bd, t, a.shape[-1])
    oa_s = _dattn_sample(page_table, lams, g_head.reshape(1, DV_A), b3(qa_s), b3(ka_s), b3(va_s),
                         cache_k[l].reshape(n_pool, PAGE_SIZE * H_A, 2 * DH_A),
                         cache_v[l].reshape(n_pool, PAGE_SIZE * H_A, DV_A),
                         pp=_pick(page_table.shape[1], 16), lam_init=lam_init)
    orr_s, ret_s = _retention(b3(qr_s), b3(kr_s), b3(vr_s), b3(gr_s), state_ret[l], _retention_tables(ret_c_s),
                              c=ret_c_s, cps=_pick(t // ret_c_s, 4), mm=F32, out_dtype=F32)
    h1_s, qm_s = _mix(oa_s.reshape(n_s, VA_W), orr_s.reshape(n_s, VR_W), x_sample.reshape(n_s, d), w_out_b,
                      row(g_post_mix), row(g_pre_mem), w_mq_b, tm=tm_s, act_dtype=F32)
    om_s = _memattn(b3(qm_s), mem_rows(cache_mem_k[l]), mem_rows(cache_mem_v[l]), tq=t, nb=_pick(bd, 4), mm=F32)
    prev = state_conv[l]
    zeros = jnp.zeros((bd, t - 2, d_ff), F32)
    p1 = jnp.concatenate([prev[:, 1:2], jnp.zeros((bd, 1, d_ff), F32), zeros], axis=1).reshape(n_s, d_ff)
    p2 = jnp.concatenate([prev, zeros], axis=1).reshape(n_s, d_ff)
    tm_fs = _pick(n_s, 256)
    y_s, a_s = _ffn(h1_s, om_s.reshape(n_s, d), w_mo_b, row(g_post_mem), row(g_pre_ffn), row(g_post_ffn),
                    w_gate_b, w_up_b, conv_w[l], row(conv_b), w_down_b, (p1, p2),
                    tm=tm_fs, seq=t, fw=d_ff // 2, carry=False)
    conv_s = a_s.reshape(bd, t, d_ff)[:, t - (CONV_W - 1):]

    return (y_p.reshape(b, s, d), y_s.reshape(bd, t, d),
            ka.reshape(1, b, s, H_A, 2 * DH_A), va.reshape(1, b, s, H_A, DV_A),
            mem_heads(mk_p)[None], mem_heads(mv_p)[None],
            ret_p[None], conv_p[None],
            ka_s.reshape(1, bd, t, H_A, 2 * DH_A), va_s.reshape(1, bd, t, H_A, DV_A),
            ret_s[None], conv_s[None])
```

```python
import functools
import math

import jax
import jax.numpy as jnp
from jax import lax
from jax.experimental import pallas as pl
from jax.experimental.pallas import tpu as pltpu

F32 = jnp.float32
BF16 = jnp.bfloat16

H_A = 4
DH_A = 64
DV_A = 2 * DH_A
H_R = 4
DK_R = 64
DV_R = 128
RET_CHUNK = 128
ROPE_BASE = 10000.0
H_M = 4
CONV_W = 3
PAGE_SIZE = 128
EPS = 1e-6
NEG = -1e30

QA_W = H_A * 2 * DH_A
KA_W = H_A * 2 * DH_A
VA_W = H_A * DV_A
QR_W = H_R * DK_R
KR_W = H_R * DK_R
VR_W = H_R * DV_R
GR_W = H_R * DV_R

VMEM_LIMIT = 56 * 1024 * 1024
SUBLANES = 8
LANES = 128

_NT = (((1,), (1,)), ((), ()))
_TN = (((0,), (0,)), ((), ()))


def _params(*sem):
    return pltpu.CompilerParams(dimension_semantics=sem, vmem_limit_bytes=VMEM_LIMIT)


def _const_spec(shape):
    nd = len(shape)
    return pl.BlockSpec(shape, lambda *_: (0,) * nd, pipeline_mode=pl.Buffered(1))


def _rms(x, g=None):
    y = x * lax.rsqrt(jnp.mean(x * x, axis=-1, keepdims=True) + EPS)
    return y if g is None else y * g


def _gelu_exact(x):
    return 0.5 * x * (1.0 + lax.erf(x * (2.0 ** -0.5)))


def _diff_lambda(lq1, lk1, lq2, lk2, lam_init):
    a = jnp.sum(lq1[...] * lk1[...], axis=-1, keepdims=True)
    b = jnp.sum(lq2[...] * lk2[...], axis=-1, keepdims=True)
    return jnp.exp(a) - jnp.exp(b) + lam_init


def _rotate(x, cos_t, sin_t):
    half = DK_R // 2
    w = x.shape[-1]
    lane = lax.broadcasted_iota(jnp.int32, x.shape, 1)
    swapped = jnp.where(lane % DK_R < half, pltpu.roll(x, w - half, 1), pltpu.roll(x, half, 1))
    return x * cos_t + swapped * sin_t


def _inproj_kernel(x_ref, g_ref, w_ref, cq_ref, sq_ref, ck_ref, sk_ref, *out_refs, prompt, tkv, q_scale):
    if prompt:
        qa_ref, ka_ref, va_ref, kab_ref, vat_ref, qr_ref, kr_ref, vr_ref, gr_ref = out_refs
    else:
        qa_ref, ka_ref, va_ref, qr_ref, kr_ref, vr_ref, gr_ref = out_refs
    xn = _rms(x_ref[...], g_ref[...]).astype(BF16)
    tm = xn.shape[0]

    def proj(c0, width):
        return jnp.dot(xn, w_ref[:, c0:c0 + width], preferred_element_type=F32)

    def store_heads(ref, val, width):
        if prompt:
            for h in range(H_A):
                ref[pl.ds(h, tm, stride=H_A), :] = val[:, h * width:(h + 1) * width]
        else:
            ref[...] = val

    c = 0
    qa = proj(c, QA_W) * q_scale
    if prompt:
        tqb = qa_ref.shape[-1]
        for r in range(tm // tqb):
            qa_ref[r] = qa[r * tqb:(r + 1) * tqb, :].T.astype(qa_ref.dtype)
    else:
        qa_ref[...] = qa.astype(qa_ref.dtype)
    c += QA_W
    ka = proj(c, KA_W)
    store_heads(ka_ref, ka, 2 * DH_A)
    c += KA_W
    va = proj(c, VA_W)
    store_heads(va_ref, va, DV_A)
    c += VA_W
    if prompt:
        kab_ref[...] = ka.astype(BF16)
        for r in range(tm // tkv):
            vat_ref[r] = va[r * tkv:(r + 1) * tkv, :].T.astype(BF16)
    qr_ref[...] = _rotate(proj(c, QR_W), cq_ref[...], sq_ref[...]).astype(qr_ref.dtype)
    c += QR_W
    kr_ref[...] = _rotate(proj(c, KR_W), ck_ref[...], sk_ref[...]).astype(kr_ref.dtype)
    c += KR_W
    vr_ref[...] = proj(c, VR_W).astype(vr_ref.dtype)
    c += VR_W
    gr_ref[...] = proj(c, GR_W)


def _inproj(x2d, g, w_in, tabs, *, tm, prompt, tkv, tqb, q_scale, act_dtype):
    n, d = x2d.shape
    period = tabs[0].shape[0]
    npb = period // tm
    row = lambda i: (i, 0)
    tab = lambda i: (i % npb, 0)
    in_specs = [pl.BlockSpec((tm, d), row), _const_spec((1, d)), _const_spec(w_in.shape)]
    in_specs += [pl.BlockSpec((tm, QR_W), tab)] * 4
    if prompt:
        out_shape = [jax.ShapeDtypeStruct((n // tqb, QA_W, tqb), act_dtype)]
        out_specs = [pl.BlockSpec((tm // tqb, QA_W, tqb), lambda i: (i, 0, 0))]
        out_shape += [jax.ShapeDtypeStruct((n * H_A, 2 * DH_A), F32),
                      jax.ShapeDtypeStruct((n * H_A, DV_A), F32),
                      jax.ShapeDtypeStruct((n, KA_W), BF16),
                      jax.ShapeDtypeStruct((n // tkv, VA_W, tkv), BF16)]
        out_specs += [pl.BlockSpec((tm * H_A, 2 * DH_A), row), pl.BlockSpec((tm * H_A, DV_A), row),
                      pl.BlockSpec((tm, KA_W), row),
                      pl.BlockSpec((tm // tkv, VA_W, tkv), lambda i: (i, 0, 0))]
    else:
        out_shape = [jax.ShapeDtypeStruct((n, QA_W), act_dtype),
                     jax.ShapeDtypeStruct((n, KA_W), F32), jax.ShapeDtypeStruct((n, VA_W), F32)]
        out_specs = [pl.BlockSpec((tm, QA_W), row), pl.BlockSpec((tm, KA_W), row), pl.BlockSpec((tm, VA_W), row)]
    out_shape += [jax.ShapeDtypeStruct((n, QR_W), act_dtype),
                  jax.ShapeDtypeStruct((n, KR_W), act_dtype),
                  jax.ShapeDtypeStruct((n, VR_W), act_dtype),
                  jax.ShapeDtypeStruct((n, GR_W), F32)]
    out_specs += [pl.BlockSpec((tm, QR_W), row), pl.BlockSpec((tm, KR_W), row),
                  pl.BlockSpec((tm, VR_W), row), pl.BlockSpec((tm, GR_W), row)]
    return pl.pallas_call(
        functools.partial(_inproj_kernel, prompt=prompt, tkv=tkv, q_scale=q_scale),
        grid=(n // tm,), in_specs=in_specs, out_specs=out_specs, out_shape=out_shape,
        compiler_params=_params("parallel"), name="inproj",
    )(x2d, g, w_in, *tabs)


def _dattn_p_body(lq1, lk1, lq2, lk2, g_ref, qt_ref, k_ref, vt_ref, o_ref, qbd_scr, st_scr, p_scr, acc_scr,
                  *, tq, tk, lam_init, after_loop):
    qi = pl.program_id(2)
    nfull = 2 * qi
    qt = qt_ref[0]
    row = lax.broadcasted_iota(jnp.int32, qt.shape, 0)
    zero = jnp.zeros_like(qt)
    qbd_scr[:, 0:tq] = jnp.where(row < DH_A, qt, zero)
    qbd_scr[:, tq:2 * tq] = jnp.where(row >= DH_A, qt, zero)
    acc_scr[...] = jnp.zeros(acc_scr.shape, F32)
    p_scr[1] = jnp.zeros(p_scr.shape[1:], BF16)

    gw = 2 * LANES
    ngroup = 2 * tq // gw

    def tile_class(d, c):
        q0 = (c * LANES) % tq
        if q0 + LANES - 1 < d * tk:
            return "skip"
        if q0 >= d * tk + tk - 1:
            return "full"
        return "mask"

    def group_live(d, g):
        return any(tile_class(d, c) != "skip" for c in range(g * gw // LANES, (g + 1) * gw // LANES))

    def scores(j, slot, g):
        kj = k_ref[pl.ds(pl.multiple_of(j * tk, tk), tk), :]
        st_scr[slot, :, g * gw:(g + 1) * gw] = jnp.dot(
            kj, qbd_scr[:, g * gw:(g + 1) * gw], preferred_element_type=F32)

    def block(j, slot, m, l, diag):
        vt_prev = vt_ref[jnp.maximum(j - 1, 0)]
        m_out, l_out = [], []
        for g in range(ngroup):
            gs = slice(g * gw, (g + 1) * gw)
            if diag is None or (diag == 0 and group_live(1, g)):
                scores(j + 1, 1 - slot, g)
            acc = acc_scr[:, gs]
            if diag != 1 or group_live(0, g):
                acc = acc + jnp.dot(vt_prev, p_scr[1 - slot, :, gs], preferred_element_type=F32)
            alphas = []
            for c in range(g * gw // LANES, (g + 1) * gw // LANES):
                cs = slice(c * LANES, (c + 1) * LANES)
                cls = "full" if diag is None else tile_class(diag, c)
                if cls == "skip":
                    m_out.append(m[:, cs])
                    l_out.append(l[:, cs])
                    continue
                st = st_scr[slot, :, cs]
                if cls == "mask":
                    kpos = diag * tk + lax.broadcasted_iota(jnp.int32, st.shape, 0)
                    qpos = (c * LANES) % tq + lax.broadcasted_iota(jnp.int32, st.shape, 1)
                    st = jnp.where(kpos <= qpos, st, NEG)
                m_new = jnp.maximum(m[:, cs], jnp.max(st, axis=0, keepdims=True))
                alpha = jnp.exp2(m[:, cs] - m_new)
                p = jnp.exp2(st - m_new)
                l_out.append(alpha * l[:, cs] + jnp.sum(p, axis=0, keepdims=True))
                m_out.append(m_new)
                alphas.append(alpha)
                p_scr[slot, :, cs] = p.astype(BF16)
            assert len(alphas) in (0, gw // LANES), "a lane group is skipped or processed as a whole"
            acc_scr[:, gs] = acc * jnp.concatenate(alphas, axis=-1) if alphas else acc
        return jnp.concatenate(m_out, axis=-1), jnp.concatenate(l_out, axis=-1)

    def pair(i, carry):
        m, l = block(2 * i, 0, carry[0], carry[1], None)
        return block(2 * i + 1, 1, m, l, None)

    for g in range(ngroup):
        scores(0, 0, g)
    carry = (jnp.full((1, 2 * tq), NEG, F32), jnp.zeros((1, 2 * tq), F32))
    carry = lax.fori_loop(0, qi, pair, carry)
    after_loop()
    carry = block(nfull, 0, carry[0], carry[1], 0)
    carry = block(nfull + 1, 1, carry[0], carry[1], 1)
    vt_last = vt_ref[nfull + 1]
    for g in range(ngroup):
        if group_live(1, g):
            gs = slice(g * gw, (g + 1) * gw)
            acc_scr[:, gs] = acc_scr[:, gs] + jnp.dot(vt_last, p_scr[1, :, gs], preferred_element_type=F32)
    lam = _diff_lambda(lq1, lk1, lq2, lk2, lam_init)
    o = acc_scr[...] * (1.0 / carry[1])
    o = o[:, :tq] - lam * o[:, tq:]
    y = o * lax.rsqrt(jnp.mean(o * o, axis=0, keepdims=True) + EPS) * g_ref[...] * (1.0 - lam_init)
    o_ref[...] = y.T.astype(o_ref.dtype)


def _decode_hooks(pt_ref, lams, g_ref, q_ref, kn_ref, vn_ref, ck_hbm, cv_hbm, o_ref,
                  kbuf, vbuf, sems, m_scr, l_scr, acc_scr, *, sidx, ns, ng, pp, t, lam_init, guard):
    slot = sidx % 2
    bi = sidx // ng
    gi = sidx % ng
    rows = PAGE_SIZE * H_A
    hm = 2 * t
    nkeys = pp * PAGE_SIZE

    def copies(step_, slot_):
        b_ = step_ // ng
        g_ = step_ % ng
        out = []
        for p in range(pp):
            page = pt_ref[b_, g_ * pp + p]
            out.append(pltpu.make_async_copy(ck_hbm.at[page], kbuf.at[slot_, pl.ds(p * rows, rows)], sems.at[0, slot_]))
            out.append(pltpu.make_async_copy(cv_hbm.at[page], vbuf.at[slot_, pl.ds(p * rows, rows)], sems.at[1, slot_]))
        return out

    def prologue():
        @pl.when(sidx == 0)
        def _():
            for c in copies(sidx, slot):
                c.start()

        @pl.when(sidx + 1 < ns)
        def _():
            for c in copies(sidx + 1, 1 - slot):
                c.start()

    def queries():
        q = q_ref[bi]
        qbd = []
        for h in range(H_A):
            qh = q[:, h * 2 * DH_A:(h + 1) * 2 * DH_A]
            lane = lax.broadcasted_iota(jnp.int32, qh.shape, 1)
            qbd.append(jnp.concatenate([jnp.where(lane < DH_A, qh, 0.0), jnp.where(lane >= DH_A, qh, 0.0)], axis=0))
        return qbd

    def update(s_all, v_heads, fresh):
        m = jnp.where(fresh, NEG, m_scr[...])
        l = jnp.where(fresh, 0.0, l_scr[...])
        acc = jnp.where(fresh, 0.0, acc_scr[...])
        m_new = jnp.maximum(m, jnp.max(s_all, axis=-1, keepdims=True))
        alpha = jnp.exp(m - m_new)
        p = jnp.exp(s_all - m_new)
        l_scr[...] = alpha * l + jnp.sum(p, axis=-1, keepdims=True)
        m_scr[...] = m_new
        pv = [jnp.dot(p[h * hm:(h + 1) * hm].astype(v_heads[h].dtype), v_heads[h], preferred_element_type=F32)
              for h in range(H_A)]
        acc_scr[...] = acc * alpha + jnp.concatenate(pv, axis=0)

    def main_body():
        for c in copies(sidx, slot):
            c.wait()
        qbd = queries()
        s_heads, v_heads = [], []
        for h in range(H_A):
            kh = kbuf[slot, pl.ds(h, nkeys, stride=H_A), :].astype(BF16)
            v_heads.append(vbuf[slot, pl.ds(h, nkeys, stride=H_A), :].astype(BF16))
            s_heads.append(lax.dot_general(qbd[h].astype(BF16), kh, _NT, preferred_element_type=F32))
        update(jnp.concatenate(s_heads, axis=0), v_heads, gi == 0)

    def main():
        if guard:
            pl.when(sidx < ns)(main_body)
        else:
            main_body()

    def epilogue_body():
        qbd = queries()
        kn = kn_ref[bi]
        vn = vn_ref[bi]
        s_heads, v_heads = [], []
        for h in range(H_A):
            s_heads.append(lax.dot_general(qbd[h], kn[:, h * 2 * DH_A:(h + 1) * 2 * DH_A], _NT,
                                           preferred_element_type=F32))
            v_heads.append(vn[:, h * DV_A:(h + 1) * DV_A])
        s_new = jnp.concatenate(s_heads, axis=0)
        kidx = lax.broadcasted_iota(jnp.int32, s_new.shape, 1)
        qidx = lax.broadcasted_iota(jnp.int32, s_new.shape, 0) % t
        update(jnp.where(kidx <= qidx, s_new, NEG), v_heads, False)

        lam = _diff_lambda(*lams, lam_init)
        o = acc_scr[...] * (1.0 / l_scr[...])
        for h in range(H_A):
            oh = o[h * hm:h * hm + t] - lam * o[h * hm + t:(h + 1) * hm]
            y = _rms(oh, g_ref[...]) * (1.0 - lam_init)
            o_ref[bi, :, h * DV_A:(h + 1) * DV_A] = y.astype(o_ref.dtype)

    def epilogue():
        last = gi == ng - 1
        pl.when(jnp.logical_and(last, sidx < ns) if guard else last)(epilogue_body)

    return prologue, main, epilogue


def _dattn_kernel(pt_ref, lq1, lk1, lq2, lk2, gcol_ref, qt_ref, k_ref, vt_ref, grow_ref, qs_ref, kn_ref, vn_ref,
                  ck_hbm, cv_hbm, o_ref, os_ref, qbd_scr, st_scr, p_scr, acc_scr, kbuf, vbuf, sems, m_scr, l_scr,
                  accs_scr, *, tq, tk, pp, t, ns, ng, guard, lam_init):
    step = (pl.program_id(0) * pl.num_programs(1) + pl.program_id(1)) * pl.num_programs(2) + pl.program_id(2)
    lams = (lq1, lk1, lq2, lk2)
    prologue, decode, epilogue = _decode_hooks(
        pt_ref, lams, grow_ref, qs_ref, kn_ref, vn_ref, ck_hbm, cv_hbm, os_ref, kbuf, vbuf, sems, m_scr, l_scr,
        accs_scr, sidx=step, ns=ns, ng=ng, pp=pp, t=t, lam_init=lam_init, guard=guard)
    prologue()
    _dattn_p_body(*lams, gcol_ref, qt_ref, k_ref, vt_ref, o_ref, qbd_scr, st_scr, p_scr, acc_scr,
                  tq=tq, tk=tk, lam_init=lam_init, after_loop=decode)
    epilogue()


def _dattn(page_table, lams, g_head, qat, kab, vat, qa_s, kn, vn, ck, cv, *, b, s, tq, tk, pp, lam_init):
    n = kab.shape[0]
    hw = 2 * DH_A
    nq = s // tq
    assert tq == 2 * tk and s % tq == 0, "the kernel walks key blocks in pairs"
    assert qat.shape == (n // tq, QA_W, tq)
    bd, t, _ = qa_s.shape
    ng = page_table.shape[1] // pp
    ns = bd * ng
    nsteps = b * H_A * nq
    assert ns <= nsteps, "one decode step per prompt grid step"
    rows = PAGE_SIZE * H_A
    cst2 = lambda bi, h, qi, pt: (0, 0)
    cst3 = lambda bi, h, qi, pt: (0, 0, 0)
    resident = lambda w: pl.BlockSpec((bd, t, w), cst3, pipeline_mode=pl.Buffered(1))
    grid_spec = pltpu.PrefetchScalarGridSpec(
        num_scalar_prefetch=1,
        grid=(b, H_A, nq),
        in_specs=[pl.BlockSpec((1, DH_A), cst2)] * 4 + [
            pl.BlockSpec((DV_A, 1), cst2),
            pl.BlockSpec((1, hw, tq), lambda bi, h, qi, pt: (bi * nq + qi, h, 0)),
            pl.BlockSpec((s, hw), lambda bi, h, qi, pt: (bi, h)),
            pl.BlockSpec((s // tk, DV_A, tk), lambda bi, h, qi, pt: (bi, h, 0)),
            pl.BlockSpec((1, DV_A), cst2),
            resident(QA_W), resident(KA_W), resident(VA_W),
            pl.BlockSpec(memory_space=pl.ANY),
            pl.BlockSpec(memory_space=pl.ANY),
        ],
        out_specs=[pl.BlockSpec((tq, DV_A), lambda bi, h, qi, pt: (bi * nq + qi, h)),
                   pl.BlockSpec((bd, t, VA_W), cst3)],
        scratch_shapes=[
            pltpu.VMEM((hw, 2 * tq), BF16), pltpu.VMEM((2, tk, 2 * tq), F32),
            pltpu.VMEM((2, tk, 2 * tq), BF16), pltpu.VMEM((DV_A, 2 * tq), F32),
            pltpu.VMEM((2, pp * rows, 2 * DH_A), F32),
            pltpu.VMEM((2, pp * rows, DV_A), F32),
            pltpu.SemaphoreType.DMA((2, 2)),
            pltpu.VMEM((H_A * 2 * t, 1), F32),
            pltpu.VMEM((H_A * 2 * t, 1), F32),
            pltpu.VMEM((H_A * 2 * t, DV_A), F32),
        ],
    )
    return pl.pallas_call(
        functools.partial(_dattn_kernel, tq=tq, tk=tk, pp=pp, t=t, ns=ns, ng=ng, guard=ns < nsteps,
                          lam_init=lam_init),
        grid_spec=grid_spec,
        out_shape=[jax.ShapeDtypeStruct((n, VA_W), BF16), jax.ShapeDtypeStruct((bd, t, VA_W), F32)],
        compiler_params=_params("arbitrary", "arbitrary", "arbitrary"), name="dattn",
    )(page_table, *lams, g_head.reshape(DV_A, 1), qat, kab, vat, g_head.reshape(1, DV_A), qa_s, kn, vn, ck, cv)


def _ret_kernel(q_ref, k_ref, v_ref, g_ref, s0_ref, dmat_ref, qdec_ref, kdec_ref, sdec_ref,
                o_ref, sout_ref, s_scr, *, c, cps, mm):
    ci = pl.program_id(1)

    @pl.when(ci == 0)
    def _():
        s_scr[...] = s0_ref[0].reshape(s_scr.shape)

    head_of_lane = lax.broadcasted_iota(jnp.int32, (c, QR_W), 1) // DK_R
    for step in range(cps):
        r0 = step * c
        q = q_ref[0, r0:r0 + c, :]
        k = k_ref[0, r0:r0 + c, :]
        v = v_ref[0, r0:r0 + c, :].astype(mm)
        gate = g_ref[0, r0:r0 + c, :]
        s_old = s_scr[...]
        s_mm = s_old.astype(mm)
        k_mm = k.astype(mm)
        kd = (k.astype(F32) * kdec_ref[...]).astype(mm)
        upd = lax.dot_general(kd, v, _TN, preferred_element_type=F32)
        for h in range(H_R):
            qh = jnp.where(head_of_lane == h, q, jnp.zeros_like(q)).astype(mm)
            vh = v[:, h * DV_R:(h + 1) * DV_R]
            s = lax.dot_general(qh, k_mm, _NT, preferred_element_type=F32) * dmat_ref[h]
            intra = jnp.dot(s.astype(mm), vh, preferred_element_type=F32)
            cross = jnp.dot(qh, s_mm, preferred_element_type=F32) * qdec_ref[:, h * DV_R:(h + 1) * DV_R]
            gh = gate[:, h * DV_R:(h + 1) * DV_R]
            y = _rms(intra + cross) * (gh * jax.nn.sigmoid(gh))
            o_ref[0, r0:r0 + c, h * DV_R:(h + 1) * DV_R] = y.astype(o_ref.dtype)
        diag = [upd[h * DK_R:(h + 1) * DK_R, h * DV_R:(h + 1) * DV_R] for h in range(H_R)]
        s_scr[...] = s_old * sdec_ref[...] + jnp.concatenate(diag, axis=0)

    @pl.when(ci == pl.num_programs(1) - 1)
    def _():
        sout_ref[0] = s_scr[...].reshape(sout_ref.shape[1:])


def _retention(qr, kr, vr, gr, s0, tabs, *, c, cps, mm, out_dtype):
    b, t, _ = qr.shape
    dmat, qdec, kdec, sdec = tabs
    tt = c * cps
    blk = lambda bi, ci: (bi, ci, 0)
    st = lambda bi, ci: (bi, 0, 0, 0)
    return pl.pallas_call(
        functools.partial(_ret_kernel, c=c, cps=cps, mm=mm),
        grid=(b, t // tt),
        in_specs=[pl.BlockSpec((1, tt, QR_W), blk), pl.BlockSpec((1, tt, KR_W), blk),
                  pl.BlockSpec((1, tt, VR_W), blk), pl.BlockSpec((1, tt, GR_W), blk),
                  pl.BlockSpec((1, H_R, DK_R, DV_R), st),
                  _const_spec(dmat.shape), _const_spec(qdec.shape), _const_spec(kdec.shape),
                  _const_spec(sdec.shape)],
        out_specs=[pl.BlockSpec((1, tt, VR_W), blk), pl.BlockSpec((1, H_R, DK_R, DV_R), st)],
        out_shape=[jax.ShapeDtypeStruct((b, t, VR_W), out_dtype),
                   jax.ShapeDtypeStruct((b, H_R, DK_R, DV_R), F32)],
        scratch_shapes=[pltpu.VMEM((H_R * DK_R, DV_R), F32)],
        compiler_params=_params("parallel", "arbitrary"), name="retention",
    )(qr, kr, vr, gr, s0, dmat, qdec, kdec, sdec)


def _mix_kernel(oa_ref, or_ref, h_ref, wo_ref, gpost_ref, gpre_ref, wq_ref, h1_ref, qm_ref, *, dh_m):
    mix_in = jnp.concatenate([oa_ref[...], or_ref[...]], axis=-1).astype(BF16)
    mix = jnp.dot(mix_in, wo_ref[...], preferred_element_type=F32)
    h1 = h_ref[...] + _rms(mix, gpost_ref[...])
    h1_ref[...] = h1
    xn = _rms(h1, gpre_ref[...]).astype(BF16)
    qm_ref[...] = (jnp.dot(xn, wq_ref[...], preferred_element_type=F32) * dh_m ** -0.5).astype(qm_ref.dtype)


def _mix(oa, orr, h, w_out, g_post, g_pre, w_mq, *, tm, act_dtype):
    n, d = h.shape
    row = lambda i: (i, 0)
    return pl.pallas_call(
        functools.partial(_mix_kernel, dh_m=d // H_M),
        grid=(n // tm,),
        in_specs=[pl.BlockSpec((tm, VA_W), row), pl.BlockSpec((tm, VR_W), row), pl.BlockSpec((tm, d), row),
                  _const_spec(w_out.shape), _const_spec((1, d)), _const_spec((1, d)), _const_spec(w_mq.shape)],
        out_specs=[pl.BlockSpec((tm, d), row), pl.BlockSpec((tm, d), row)],
        out_shape=[jax.ShapeDtypeStruct((n, d), F32), jax.ShapeDtypeStruct((n, d), act_dtype)],
        compiler_params=_params("parallel"), name="mix_out",
    )(oa, orr, h, w_out, g_post, g_pre, w_mq)


def _mem_chunk(h, c):
    return c * H_M + h


def _memkv_kernel(mem_ref, g_ref, wk_ref, wv_ref, mk_ref, mv_ref):
    mn = _rms(mem_ref[...], g_ref[...]).astype(BF16)
    tm, d = mn.shape
    halves = d // H_M // LANES
    nchunk = H_M * halves
    for w_ref, o_ref in ((wk_ref, mk_ref), (wv_ref, mv_ref)):
        y = jnp.dot(mn, w_ref[...], preferred_element_type=F32)
        for h in range(H_M):
            for c in range(halves):
                col = (h * halves + c) * LANES
                o_ref[pl.ds(_mem_chunk(h, c), tm, stride=nchunk), :] = y[:, col:col + LANES]


def _memkv(mem2d, g, w_mk, w_mv, *, tm):
    n, d = mem2d.shape
    nchunk = d // LANES
    row = lambda i: (i, 0)
    return pl.pallas_call(
        _memkv_kernel, grid=(n // tm,),
        in_specs=[pl.BlockSpec((tm, d), row), _const_spec((1, d)), _const_spec(w_mk.shape), _const_spec(w_mv.shape)],
        out_specs=[pl.BlockSpec((tm * nchunk, LANES), row)] * 2,
        out_shape=[jax.ShapeDtypeStruct((n * nchunk, LANES), F32)] * 2,
        compiler_params=_params("parallel"), name="mem_kv",
    )(mem2d, g, w_mk, w_mv)


def _memattn_kernel(q_ref, mk_ref, mv_ref, o_ref, *, mm, nb):
    d = q_ref.shape[-1]
    dh = d // H_M
    halves = dh // LANES
    nchunk = H_M * halves
    n_mem = mk_ref.shape[1] // nchunk

    def head(ref, bi, h):
        parts = [ref[bi, pl.ds(_mem_chunk(h, c), n_mem, stride=nchunk), :] for c in range(halves)]
        return jnp.concatenate(parts, axis=-1).astype(mm)

    for bi in range(nb):
        q = q_ref[bi]
        for h in range(H_M):
            qh = q[:, h * dh:(h + 1) * dh].astype(mm)
            s = lax.dot_general(qh, head(mk_ref, bi, h), _NT, preferred_element_type=F32)
            p = jnp.exp(s - jnp.max(s, axis=-1, keepdims=True))
            p = p * (1.0 / jnp.sum(p, axis=-1, keepdims=True))
            o = jnp.dot(p.astype(mm), head(mv_ref, bi, h), preferred_element_type=F32)
            o_ref[bi, :, h * dh:(h + 1) * dh] = o.astype(o_ref.dtype)


def _memattn(qm, mk, mv, *, tq, nb, mm):
    b, t, d = qm.shape
    rows = mk.shape[1]
    return pl.pallas_call(
        functools.partial(_memattn_kernel, mm=mm, nb=nb),
        grid=(b // nb, t // tq),
        in_specs=[pl.BlockSpec((nb, tq, d), lambda bi, qi: (bi, qi, 0)),
                  pl.BlockSpec((nb, rows, LANES), lambda bi, qi: (bi, 0, 0)),
                  pl.BlockSpec((nb, rows, LANES), lambda bi, qi: (bi, 0, 0))],
        out_specs=pl.BlockSpec((nb, tq, d), lambda bi, qi: (bi, qi, 0)),
        out_shape=jax.ShapeDtypeStruct((b, t, d), qm.dtype),
        compiler_params=_params("parallel", "arbitrary"), name="mem_attn",
    )(qm, mk, mv)


def _ffn_kernel(h1_ref, o_ref, wmo_ref, gpm_ref, gpre_ref, gpost_ref, wg_ref, wu_ref, cw_ref, cb_ref, wd_ref,
                *rest, tm, seq, fw, carry):
    if carry:
        prev_ref, y_ref, cnew_ref, a_scr, h_scr = rest
    else:
        p1_ref, p2_ref, y_ref, a_ref, a_scr, h_scr = rest
    i = pl.program_id(0)
    pad = SUBLANES
    m = jnp.dot(o_ref[...].astype(BF16), wmo_ref[...], preferred_element_type=F32)
    h2 = h1_ref[...] + _rms(m, gpm_ref[...])
    xn = _rms(h2, gpre_ref[...]).astype(BF16)
    d_ff = wg_ref.shape[1]

    if carry:
        @pl.when(i % (seq // tm) == 0)
        def _():
            a_scr[pad - 2:pad, :] = prev_ref[0]
    else:
        a_scr[0:pad, :] = jnp.zeros((pad, d_ff), F32)
        tpos = lax.broadcasted_iota(jnp.int32, (tm, fw), 0) % seq

    for c0 in range(0, d_ff, fw):
        cs = slice(c0, c0 + fw)
        a = jnp.dot(xn, wg_ref[:, cs], preferred_element_type=F32)
        a_scr[pad:pad + tm, cs] = a
        s1 = a_scr[pad - 1:pad - 1 + tm, cs]
        s2 = a_scr[pad - 2:pad - 2 + tm, cs]
        if not carry:
            s1 = jnp.where(tpos == 0, p1_ref[:, cs], s1)
            s2 = jnp.where(tpos < 2, p2_ref[:, cs], s2)
        conv = cb_ref[:, cs] + s2 * cw_ref[0:1, cs] + s1 * cw_ref[1:2, cs] + a * cw_ref[2:3, cs]
        up = jnp.dot(xn, wu_ref[:, cs], preferred_element_type=F32)
        h_scr[:, cs] = (_gelu_exact(conv) * up).astype(BF16)

    if carry:
        cnew_ref[0] = a_scr[pad + tm - 2:pad + tm, :]
        a_scr[0:pad, :] = a_scr[tm:tm + pad, :]
    else:
        a_ref[...] = a_scr[pad:pad + tm, :]
    f = jnp.dot(h_scr[...], wd_ref[...], preferred_element_type=F32)
    y_ref[...] = h2 + _rms(f, gpost_ref[...])


def _ffn(h1, o, w_mo, g_post_mem, g_pre, g_post, w_gate, w_up, conv_w, conv_b, w_down, prev, *, tm, seq, fw, carry):
    n, d = h1.shape
    d_ff = w_gate.shape[1]
    row = lambda i: (i, 0)
    in_specs = [pl.BlockSpec((tm, d), row), pl.BlockSpec((tm, d), row), _const_spec(w_mo.shape),
                _const_spec((1, d)), _const_spec((1, d)), _const_spec((1, d)),
                _const_spec(w_gate.shape), _const_spec(w_up.shape), _const_spec(conv_w.shape),
                _const_spec((1, d_ff)), _const_spec(w_down.shape)]
    out_specs = [pl.BlockSpec((tm, d), row)]
    out_shape = [jax.ShapeDtypeStruct((n, d), F32)]
    if carry:
        tps = seq // tm
        in_specs += [pl.BlockSpec((1, CONV_W - 1, d_ff), lambda i: (i // tps, 0, 0))]
        out_specs += [pl.BlockSpec((1, CONV_W - 1, d_ff), lambda i: (i // tps, 0, 0))]
        out_shape += [jax.ShapeDtypeStruct((n // seq, CONV_W - 1, d_ff), F32)]
        extra = (prev,)
    else:
        in_specs += [pl.BlockSpec((tm, d_ff), row)] * 2
        out_specs += [pl.BlockSpec((tm, d_ff), row)]
        out_shape += [jax.ShapeDtypeStruct((n, d_ff), F32)]
        extra = prev
    return pl.pallas_call(
        functools.partial(_ffn_kernel, tm=tm, seq=seq, fw=fw, carry=carry),
        grid=(n // tm,), in_specs=in_specs, out_specs=out_specs, out_shape=out_shape,
        scratch_shapes=[pltpu.VMEM((tm + SUBLANES, d_ff), F32), pltpu.VMEM((tm, d_ff), BF16)],
        compiler_params=_params("arbitrary"), name="mem_out_ffn",
    )(h1, o, w_mo, g_post_mem, g_pre, g_post, w_gate, w_up, conv_w, conv_b, w_down, *extra)


def _rotary_tables(pos, k_scale):
    half = DK_R // 2
    inv = 1.0 / (ROPE_BASE ** jnp.linspace(0.0, 1.0, half, dtype=F32))
    ang = pos.astype(F32)[:, None] * inv[None, :]
    cos, sin = jnp.cos(ang), jnp.sin(ang)
    cos_t = jnp.tile(jnp.concatenate([cos, cos], axis=-1), (1, H_R))
    sin_t = jnp.tile(jnp.concatenate([-sin, sin], axis=-1), (1, H_R))
    return cos_t, sin_t, cos_t * k_scale, sin_t * k_scale


def _retention_tables(c):
    log_g = jnp.log(1.0 - 2.0 ** (-5.0 - jnp.arange(H_R, dtype=F32)))
    i = jnp.arange(c, dtype=F32)
    diff = i[:, None] - i[None, :]
    dmat = jnp.where(diff[None] >= 0, jnp.exp(jnp.maximum(diff, 0.0)[None] * log_g[:, None, None]), 0.0)
    qdec = jnp.exp((i + 1.0)[:, None] * log_g[None, :])
    kdec = jnp.exp((c - 1.0 - i)[:, None] * log_g[None, :])
    sdec = jnp.exp(c * log_g)
    return (dmat, jnp.repeat(qdec, DV_R, axis=1), jnp.repeat(kdec, DK_R, axis=1),
            jnp.broadcast_to(jnp.repeat(sdec, DK_R)[:, None], (H_R * DK_R, DV_R)))


def _pick(n, pref):
    t = min(n, pref)
    while n % t:
        t //= 2
    return t


def kernel(x_prompt, x_sample, cache_k, cache_v, cache_mem_k, cache_mem_v, state_ret, state_conv, page_table, mem_prompt, w_in, w_out, lam_q1, lam_k1, lam_q2, lam_k2, g_diff_head, g_pre_mix, g_post_mix, g_pre_mem, g_post_mem, g_mem_in, w_mq, w_mk, w_mv, w_mo, g_pre_ffn, g_post_ffn, w_gate, w_up, conv_w, conv_b, w_down):
    b, s, d = x_prompt.shape
    bd, t, _ = x_sample.shape
    depth = w_in.shape[0]
    n_mem = mem_prompt.shape[1]
    d_ff = w_gate.shape[-1]
    past = page_table.shape[1] * PAGE_SIZE
    n_pool = cache_k.shape[1]
    assert depth == 1, "the sample path keeps the layer loop unrolled for a single layer"
    l = 0
    lam_init = 0.8 - 0.6 * math.exp(-0.3 * l)

    row = lambda v: v[l].reshape(1, -1)
    bf = lambda w: w[l].astype(BF16)
    w_in_b, w_out_b, w_mq_b, w_mk_b, w_mv_b, w_mo_b = map(bf, (w_in, w_out, w_mq, w_mk, w_mv, w_mo))
    w_gate_b, w_up_b, w_down_b = map(bf, (w_gate, w_up, w_down))
    lams = tuple(row(v) for v in (lam_q1, lam_k1, lam_q2, lam_k2))
    g_head = g_diff_head[l]

    tm_p = _pick(s, 512)
    tkv = _pick(s, 256)
    tq = _pick(s, 512)
    ret_c = RET_CHUNK if s % RET_CHUNK == 0 else s
    ret_c_s = RET_CHUNK if t % RET_CHUNK == 0 else t
    n_s = bd * t
    dh_m = d // H_M
    halves = dh_m // LANES
    nchunk = H_M * halves

    def mem_rows(a):
        g = a.shape[0]
        return a.reshape(g, n_mem, H_M, halves, LANES).transpose(0, 1, 3, 2, 4).reshape(g, n_mem * nchunk, LANES)

    def mem_heads(a):
        g = a.shape[0]
        return a.reshape(g, n_mem, halves, H_M, LANES).transpose(0, 1, 3, 2, 4).reshape(g, n_mem, H_M, dh_m)

    mk_p, mv_p = _memkv(mem_prompt.reshape(b * n_mem, d), row(g_mem_in), w_mk_b, w_mv_b, tm=_pick(b * n_mem, 512))
    mk_p = mk_p.reshape(b, n_mem * nchunk, LANES)
    mv_p = mv_p.reshape(b, n_mem * nchunk, LANES)
    tabs_p = _rotary_tables(jnp.arange(s), DK_R ** -0.5)
    qat, ka, va, kab, vat, qr, kr, vr, gr = _inproj(
        x_prompt.reshape(b * s, d), row(g_pre_mix), w_in_b, tabs_p, tm=tm_p, prompt=True, tkv=tkv, tqb=tq,
        q_scale=DH_A ** -0.5 * math.log2(math.e), act_dtype=BF16)
    tm_s = _pick(n_s, 512)
    reps = tm_s // t
    tabs_s = tuple(jnp.tile(a, (reps, 1)) for a in _rotary_tables(past + jnp.arange(t), DK_R ** -0.5))
    qa_s, ka_s, va_s, qr_s, kr_s, vr_s, gr_s = _inproj(
        x_sample.reshape(n_s, d), row(g_pre_mix), w_in_b, tabs_s, tm=tm_s, prompt=False, tkv=tkv, tqb=tq,
        q_scale=DH_A ** -0.5, act_dtype=F32)
    b3 = lambda a: a.reshape(bd, t, a.shape[-1])
    oa, oa_s = _dattn(page_table, lams, g_head, qat, kab, vat, b3(qa_s), b3(ka_s), b3(va_s),
                      cache_k[l].reshape(n_pool, PAGE_SIZE * H_A, 2 * DH_A),
                      cache_v[l].reshape(n_pool, PAGE_SIZE * H_A, DV_A),
                      b=b, s=s, tq=tq, tk=tkv, pp=_pick(page_table.shape[1], 16), lam_init=lam_init)
    r3 = lambda a: a.reshape(b, s, a.shape[-1])
    orr, ret_p = _retention(r3(qr), r3(kr), r3(vr), r3(gr), jnp.zeros((b, H_R, DK_R, DV_R), F32),
                            _retention_tables(ret_c), c=ret_c, cps=_pick(s // ret_c, 4), mm=BF16, out_dtype=BF16)
    h1, qm = _mix(oa, orr.reshape(b * s, VR_W), x_prompt.reshape(b * s, d), w_out_b, row(g_post_mix),
                  row(g_pre_mem), w_mq_b, tm=tm_p, act_dtype=BF16)
    om = _memattn(qm.reshape(b, s, d), mk_p, mv_p, tq=tm_p, nb=1, mm=BF16)
    tm_f = _pick(s, 256)
    y_p, conv_p = _ffn(h1, om.reshape(b * s, d), w_mo_b, row(g_post_mem), row(g_pre_ffn), row(g_post_ffn),
                       w_gate_b, w_up_b, conv_w[l], row(conv_b), w_down_b,
                       jnp.zeros((b, CONV_W - 1, d_ff), F32), tm=tm_f, seq=s, fw=d_ff // 2, carry=True)

    orr_s, ret_s = _retention(b3(qr_s), b3(kr_s), b3(vr_s), b3(gr_s), state_ret[l], _retention_tables(ret_c_s),
                              c=ret_c_s, cps=_pick(t // ret_c_s, 4), mm=F32, out_dtype=F32)
    h1_s, qm_s = _mix(oa_s.reshape(n_s, VA_W), orr_s.reshape(n_s, VR_W), x_sample.reshape(n_s, d), w_out_b,
                      row(g_post_mix), row(g_pre_mem), w_mq_b, tm=tm_s, act_dtype=F32)
    om_s = _memattn(b3(qm_s), mem_rows(cache_mem_k[l]), mem_rows(cache_mem_v[l]), tq=t, nb=_pick(bd, 4), mm=F32)
    prev = state_conv[l]
    zeros = jnp.zeros((bd, t - 2, d_ff), F32)
    p1 = jnp.concatenate([prev[:, 1:2], jnp.zeros((bd, 1, d_ff), F32), zeros], axis=1).reshape(n_s, d_ff)
    p2 = jnp.concatenate([prev, zeros], axis=1).reshape(n_s, d_ff)
    tm_fs = _pick(n_s, 256)
    y_s, a_s = _ffn(h1_s, om_s.reshape(n_s, d), w_mo_b, row(g_post_mem), row(g_pre_ffn), row(g_post_ffn),
                    w_gate_b, w_up_b, conv_w[l], row(conv_b), w_down_b, (p1, p2),
                    tm=tm_fs, seq=t, fw=d_ff // 2, carry=False)
    conv_s = a_s.reshape(bd, t, d_ff)[:, t - (CONV_W - 1):]

    return (y_p.reshape(b, s, d), y_s.reshape(bd, t, d),
            ka.reshape(1, b, s, H_A, 2 * DH_A), va.reshape(1, b, s, H_A, DV_A),
            mem_heads(mk_p)[None], mem_heads(mv_p)[None],
            ret_p[None], conv_p[None],
            ka_s.reshape(1, bd, t, H_A, 2 * DH_A), va_s.reshape(1, bd, t, H_A, DV_A),
            ret_s[None], conv_s[None])
```

```python
import functools
import math

import jax
import jax.numpy as jnp
from jax import lax
from jax.experimental import pallas as pl
from jax.experimental.pallas import tpu as pltpu

F32 = jnp.float32
BF16 = jnp.bfloat16

H_A = 4
DH_A = 64
DV_A = 2 * DH_A
H_R = 4
DK_R = 64
DV_R = 128
RET_CHUNK = 128
ROPE_BASE = 10000.0
H_M = 4
CONV_W = 3
PAGE_SIZE = 128
EPS = 1e-6
NEG = -1e30

QA_W = H_A * 2 * DH_A
KA_W = H_A * 2 * DH_A
VA_W = H_A * DV_A
QR_W = H_R * DK_R
KR_W = H_R * DK_R
VR_W = H_R * DV_R
GR_W = H_R * DV_R

VMEM_LIMIT = 56 * 1024 * 1024
SUBLANES = 8
LANES = 128
SUM_ROWS = 16
PV_ROWS = DV_A + SUM_ROWS

_NT = (((1,), (1,)), ((), ()))
_TN = (((0,), (0,)), ((), ()))


def _params(*sem):
    return pltpu.CompilerParams(dimension_semantics=sem, vmem_limit_bytes=VMEM_LIMIT)


def _const_spec(shape):
    nd = len(shape)
    return pl.BlockSpec(shape, lambda *_: (0,) * nd, pipeline_mode=pl.Buffered(1))


def _rms(x, g=None):
    y = x * lax.rsqrt(jnp.mean(x * x, axis=-1, keepdims=True) + EPS)
    return y if g is None else y * g


def _gelu_exact(x):
    return 0.5 * x * (1.0 + lax.erf(x * (2.0 ** -0.5)))


def _diff_lambda(lq1, lk1, lq2, lk2, lam_init):
    a = jnp.sum(lq1[...] * lk1[...], axis=-1, keepdims=True)
    b = jnp.sum(lq2[...] * lk2[...], axis=-1, keepdims=True)
    return jnp.exp(a) - jnp.exp(b) + lam_init


def _rotate(x, cos_t, sin_t):
    half = DK_R // 2
    w = x.shape[-1]
    lane = lax.broadcasted_iota(jnp.int32, x.shape, 1)
    swapped = jnp.where(lane % DK_R < half, pltpu.roll(x, w - half, 1), pltpu.roll(x, half, 1))
    return x * cos_t + swapped * sin_t


def _inproj_kernel(x_ref, g_ref, w_ref, cq_ref, sq_ref, ck_ref, sk_ref, *out_refs, prompt, tkv, q_scale):
    if prompt:
        qa_ref, ka_ref, va_ref, kab_ref, vat_ref, qr_ref, kr_ref, vr_ref, gr_ref = out_refs
    else:
        qa_ref, ka_ref, va_ref, qr_ref, kr_ref, vr_ref, gr_ref = out_refs
    xn = _rms(x_ref[...], g_ref[...]).astype(BF16)
    tm = xn.shape[0]

    def proj(c0, width):
        return jnp.dot(xn, w_ref[:, c0:c0 + width], preferred_element_type=F32)

    def store_heads(ref, val, width):
        if prompt:
            for h in range(H_A):
                ref[pl.ds(h, tm, stride=H_A), :] = val[:, h * width:(h + 1) * width]
        else:
            ref[...] = val

    c = 0
    qa = proj(c, QA_W) * q_scale
    if prompt:
        tqb = qa_ref.shape[-1]
        for r in range(tm // tqb):
            qa_ref[r] = qa[r * tqb:(r + 1) * tqb, :].T.astype(qa_ref.dtype)
    else:
        qa_ref[...] = qa.astype(qa_ref.dtype)
    c += QA_W
    ka = proj(c, KA_W)
    store_heads(ka_ref, ka, 2 * DH_A)
    c += KA_W
    va = proj(c, VA_W)
    store_heads(va_ref, va, DV_A)
    c += VA_W
    if prompt:
        kab_ref[...] = ka.astype(BF16)
        ones = jnp.ones((SUM_ROWS, tkv), BF16)
        for r in range(tm // tkv):
            vt = va[r * tkv:(r + 1) * tkv, :].T.astype(BF16)
            for h in range(H_A):
                vat_ref[r, h * PV_ROWS:h * PV_ROWS + DV_A, :] = vt[h * DV_A:(h + 1) * DV_A, :]
                vat_ref[r, h * PV_ROWS + DV_A:(h + 1) * PV_ROWS, :] = ones
    qr_ref[...] = _rotate(proj(c, QR_W), cq_ref[...], sq_ref[...]).astype(qr_ref.dtype)
    c += QR_W
    kr_ref[...] = _rotate(proj(c, KR_W), ck_ref[...], sk_ref[...]).astype(kr_ref.dtype)
    c += KR_W
    vr_ref[...] = proj(c, VR_W).astype(vr_ref.dtype)
    c += VR_W
    gr_ref[...] = proj(c, GR_W)


def _inproj(x2d, g, w_in, tabs, *, tm, prompt, tkv, tqb, q_scale, act_dtype):
    n, d = x2d.shape
    period = tabs[0].shape[0]
    npb = period // tm
    row = lambda i: (i, 0)
    tab = lambda i: (i % npb, 0)
    in_specs = [pl.BlockSpec((tm, d), row), _const_spec((1, d)), _const_spec(w_in.shape)]
    in_specs += [pl.BlockSpec((tm, QR_W), tab)] * 4
    if prompt:
        out_shape = [jax.ShapeDtypeStruct((n // tqb, QA_W, tqb), act_dtype)]
        out_specs = [pl.BlockSpec((tm // tqb, QA_W, tqb), lambda i: (i, 0, 0))]
        out_shape += [jax.ShapeDtypeStruct((n * H_A, 2 * DH_A), F32),
                      jax.ShapeDtypeStruct((n * H_A, DV_A), F32),
                      jax.ShapeDtypeStruct((n, KA_W), BF16),
                      jax.ShapeDtypeStruct((n // tkv, H_A * PV_ROWS, tkv), BF16)]
        out_specs += [pl.BlockSpec((tm * H_A, 2 * DH_A), row), pl.BlockSpec((tm * H_A, DV_A), row),
                      pl.BlockSpec((tm, KA_W), row),
                      pl.BlockSpec((tm // tkv, H_A * PV_ROWS, tkv), lambda i: (i, 0, 0))]
    else:
        out_shape = [jax.ShapeDtypeStruct((n, QA_W), act_dtype),
                     jax.ShapeDtypeStruct((n, KA_W), F32), jax.ShapeDtypeStruct((n, VA_W), F32)]
        out_specs = [pl.BlockSpec((tm, QA_W), row), pl.BlockSpec((tm, KA_W), row), pl.BlockSpec((tm, VA_W), row)]
    out_shape += [jax.ShapeDtypeStruct((n, QR_W), act_dtype),
                  jax.ShapeDtypeStruct((n, KR_W), act_dtype),
                  jax.ShapeDtypeStruct((n, VR_W), act_dtype),
                  jax.ShapeDtypeStruct((n, GR_W), F32)]
    out_specs += [pl.BlockSpec((tm, QR_W), row), pl.BlockSpec((tm, KR_W), row),
                  pl.BlockSpec((tm, VR_W), row), pl.BlockSpec((tm, GR_W), row)]
    return pl.pallas_call(
        functools.partial(_inproj_kernel, prompt=prompt, tkv=tkv, q_scale=q_scale),
        grid=(n // tm,), in_specs=in_specs, out_specs=out_specs, out_shape=out_shape,
        compiler_params=_params("parallel"), name="inproj",
    )(x2d, g, w_in, *tabs)


def _dattn_p_body(lq1, lk1, lq2, lk2, g_ref, qt_ref, k_ref, vt_ref, o_ref, qbd_scr, st_scr, p_scr, acc_scr,
                  *, tq, tk, lam_init, after_loop):
    qi = pl.program_id(2)
    nfull = 2 * qi
    qt = qt_ref[0]
    row = lax.broadcasted_iota(jnp.int32, qt.shape, 0)
    zero = jnp.zeros_like(qt)
    qbd_scr[:, 0:tq] = jnp.where(row < DH_A, qt, zero)
    qbd_scr[:, tq:2 * tq] = jnp.where(row >= DH_A, qt, zero)
    acc_scr[...] = jnp.zeros(acc_scr.shape, F32)
    p_scr[1] = jnp.zeros(p_scr.shape[1:], BF16)

    gw = 2 * LANES
    ngroup = 2 * tq // gw

    def tile_class(d, c):
        q0 = (c * LANES) % tq
        if q0 + LANES - 1 < d * tk:
            return "skip"
        if q0 >= d * tk + tk - 1:
            return "full"
        return "mask"

    def group_live(d, g):
        return any(tile_class(d, c) != "skip" for c in range(g * gw // LANES, (g + 1) * gw // LANES))

    def scores(j, slot, g):
        kj = k_ref[pl.ds(pl.multiple_of(j * tk, tk), tk), :]
        st_scr[slot, :, g * gw:(g + 1) * gw] = jnp.dot(
            kj, qbd_scr[:, g * gw:(g + 1) * gw], preferred_element_type=F32)

    def block(j, slot, m, diag):
        vt_prev = vt_ref[jnp.maximum(j - 1, 0)]
        m_out = []
        for g in range(ngroup):
            gs = slice(g * gw, (g + 1) * gw)
            if diag is None or (diag == 0 and group_live(1, g)):
                scores(j + 1, 1 - slot, g)
            acc = acc_scr[:, gs]
            if diag != 1 or group_live(0, g):
                acc = acc + jnp.dot(vt_prev, p_scr[1 - slot, :, gs], preferred_element_type=F32)
            alphas = []
            for c in range(g * gw // LANES, (g + 1) * gw // LANES):
                cs = slice(c * LANES, (c + 1) * LANES)
                cls = "full" if diag is None else tile_class(diag, c)
                if cls == "skip":
                    m_out.append(m[:, cs])
                    continue
                st = st_scr[slot, :, cs]
                if cls == "mask":
                    kpos = diag * tk + lax.broadcasted_iota(jnp.int32, st.shape, 0)
                    qpos = (c * LANES) % tq + lax.broadcasted_iota(jnp.int32, st.shape, 1)
                    st = jnp.where(kpos <= qpos, st, NEG)
                m_new = jnp.maximum(m[:, cs], jnp.max(st, axis=0, keepdims=True))
                alphas.append(jnp.exp2(m[:, cs] - m_new))
                m_out.append(m_new)
                p_scr[slot, :, cs] = jnp.exp2((st - m_new).astype(BF16))
            assert len(alphas) in (0, gw // LANES), "a lane group is skipped or processed as a whole"
            acc_scr[:, gs] = acc * jnp.concatenate(alphas, axis=-1) if alphas else acc
        return jnp.concatenate(m_out, axis=-1)

    def pair(i, m):
        return block(2 * i + 1, 1, block(2 * i, 0, m, None), None)

    for g in range(ngroup):
        scores(0, 0, g)
    m = lax.fori_loop(0, qi, pair, jnp.full((1, 2 * tq), NEG, F32))
    after_loop()
    m = block(nfull, 0, m, 0)
    m = block(nfull + 1, 1, m, 1)
    vt_last = vt_ref[nfull + 1]
    for g in range(ngroup):
        if group_live(1, g):
            gs = slice(g * gw, (g + 1) * gw)
            acc_scr[:, gs] = acc_scr[:, gs] + jnp.dot(vt_last, p_scr[1, :, gs], preferred_element_type=F32)
    lam = _diff_lambda(lq1, lk1, lq2, lk2, lam_init)
    o = acc_scr[0:DV_A, :] * (1.0 / acc_scr[DV_A:DV_A + 1, :])
    o = o[:, :tq] - lam * o[:, tq:]
    y = o * lax.rsqrt(jnp.mean(o * o, axis=0, keepdims=True) + EPS) * g_ref[...] * (1.0 - lam_init)
    o_ref[...] = y.T.astype(o_ref.dtype)


def _decode_hooks(pt_ref, lams, g_ref, q_ref, kn_ref, vn_ref, ck_hbm, cv_hbm, o_ref,
                  kbuf, vbuf, sems, m_scr, l_scr, acc_scr, *, sidx, ns, ng, pp, t, lam_init, guard):
    slot = sidx % 2
    bi = sidx // ng
    gi = sidx % ng
    rows = PAGE_SIZE * H_A
    hm = 2 * t
    nkeys = pp * PAGE_SIZE

    def copies(step_, slot_):
        b_ = step_ // ng
        g_ = step_ % ng
        out = []
        for p in range(pp):
            page = pt_ref[b_, g_ * pp + p]
            out.append(pltpu.make_async_copy(ck_hbm.at[page], kbuf.at[slot_, pl.ds(p * rows, rows)], sems.at[0, slot_]))
            out.append(pltpu.make_async_copy(cv_hbm.at[page], vbuf.at[slot_, pl.ds(p * rows, rows)], sems.at[1, slot_]))
        return out

    def prologue():
        @pl.when(sidx == 0)
        def _():
            for c in copies(sidx, slot):
                c.start()

        @pl.when(sidx + 1 < ns)
        def _():
            for c in copies(sidx + 1, 1 - slot):
                c.start()

    def queries():
        q = q_ref[bi]
        qbd = []
        for h in range(H_A):
            qh = q[:, h * 2 * DH_A:(h + 1) * 2 * DH_A]
            lane = lax.broadcasted_iota(jnp.int32, qh.shape, 1)
            qbd.append(jnp.concatenate([jnp.where(lane < DH_A, qh, 0.0), jnp.where(lane >= DH_A, qh, 0.0)], axis=0))
        return qbd

    def update(s_all, v_heads, fresh):
        m = jnp.where(fresh, NEG, m_scr[...])
        l = jnp.where(fresh, 0.0, l_scr[...])
        acc = jnp.where(fresh, 0.0, acc_scr[...])
        m_new = jnp.maximum(m, jnp.max(s_all, axis=-1, keepdims=True))
        alpha = jnp.exp(m - m_new)
        p = jnp.exp(s_all - m_new)
        l_scr[...] = alpha * l + jnp.sum(p, axis=-1, keepdims=True)
        m_scr[...] = m_new
        pv = [jnp.dot(p[h * hm:(h + 1) * hm].astype(v_heads[h].dtype), v_heads[h], preferred_element_type=F32)
              for h in range(H_A)]
        acc_scr[...] = acc * alpha + jnp.concatenate(pv, axis=0)

    def main_body():
        for c in copies(sidx, slot):
            c.wait()
        qbd = queries()
        s_heads, v_heads = [], []
        for h in range(H_A):
            kh = kbuf[slot, pl.ds(h, nkeys, stride=H_A), :].astype(BF16)
            v_heads.append(vbuf[slot, pl.ds(h, nkeys, stride=H_A), :].astype(BF16))
            s_heads.append(lax.dot_general(qbd[h].astype(BF16), kh, _NT, preferred_element_type=F32))
        update(jnp.concatenate(s_heads, axis=0), v_heads, gi == 0)

    def main():
        if guard:
            pl.when(sidx < ns)(main_body)
        else:
            main_body()

    def epilogue_body():
        qbd = queries()
        kn = kn_ref[bi]
        vn = vn_ref[bi]
        s_heads, v_heads = [], []
        for h in range(H_A):
            s_heads.append(lax.dot_general(qbd[h], kn[:, h * 2 * DH_A:(h + 1) * 2 * DH_A], _NT,
                                           preferred_element_type=F32))
            v_heads.append(vn[:, h * DV_A:(h + 1) * DV_A])
        s_new = jnp.concatenate(s_heads, axis=0)
        kidx = lax.broadcasted_iota(jnp.int32, s_new.shape, 1)
        qidx = lax.broadcasted_iota(jnp.int32, s_new.shape, 0) % t
        update(jnp.where(kidx <= qidx, s_new, NEG), v_heads, False)

        lam = _diff_lambda(*lams, lam_init)
        o = acc_scr[...] * (1.0 / l_scr[...])
        for h in range(H_A):
            oh = o[h * hm:h * hm + t] - lam * o[h * hm + t:(h + 1) * hm]
            y = _rms(oh, g_ref[...]) * (1.0 - lam_init)
            o_ref[bi, :, h * DV_A:(h + 1) * DV_A] = y.astype(o_ref.dtype)

    def epilogue():
        last = gi == ng - 1
        pl.when(jnp.logical_and(last, sidx < ns) if guard else last)(epilogue_body)

    return prologue, main, epilogue


def _dattn_kernel(pt_ref, lq1, lk1, lq2, lk2, gcol_ref, qt_ref, k_ref, vt_ref, grow_ref, qs_ref, kn_ref, vn_ref,
                  ck_hbm, cv_hbm, o_ref, os_ref, qbd_scr, st_scr, p_scr, acc_scr, kbuf, vbuf, sems, m_scr, l_scr,
                  accs_scr, *, tq, tk, pp, t, ns, ng, guard, lam_init):
    step = (pl.program_id(0) * pl.num_programs(1) + pl.program_id(1)) * pl.num_programs(2) + pl.program_id(2)
    lams = (lq1, lk1, lq2, lk2)
    prologue, decode, epilogue = _decode_hooks(
        pt_ref, lams, grow_ref, qs_ref, kn_ref, vn_ref, ck_hbm, cv_hbm, os_ref, kbuf, vbuf, sems, m_scr, l_scr,
        accs_scr, sidx=step, ns=ns, ng=ng, pp=pp, t=t, lam_init=lam_init, guard=guard)
    prologue()
    _dattn_p_body(*lams, gcol_ref, qt_ref, k_ref, vt_ref, o_ref, qbd_scr, st_scr, p_scr, acc_scr,
                  tq=tq, tk=tk, lam_init=lam_init, after_loop=decode)
    epilogue()


def _dattn(page_table, lams, g_head, qat, kab, vat, qa_s, kn, vn, ck, cv, *, b, s, tq, tk, pp, lam_init):
    n = kab.shape[0]
    hw = 2 * DH_A
    nq = s // tq
    assert tq == 2 * tk and s % tq == 0, "the kernel walks key blocks in pairs"
    assert qat.shape == (n // tq, QA_W, tq)
    bd, t, _ = qa_s.shape
    ng = page_table.shape[1] // pp
    ns = bd * ng
    nsteps = b * H_A * nq
    assert ns <= nsteps, "one decode step per prompt grid step"
    rows = PAGE_SIZE * H_A
    cst2 = lambda bi, h, qi, pt: (0, 0)
    cst3 = lambda bi, h, qi, pt: (0, 0, 0)
    resident = lambda w: pl.BlockSpec((bd, t, w), cst3, pipeline_mode=pl.Buffered(1))
    grid_spec = pltpu.PrefetchScalarGridSpec(
        num_scalar_prefetch=1,
        grid=(b, H_A, nq),
        in_specs=[pl.BlockSpec((1, DH_A), cst2)] * 4 + [
            pl.BlockSpec((DV_A, 1), cst2),
            pl.BlockSpec((1, hw, tq), lambda bi, h, qi, pt: (bi * nq + qi, h, 0)),
            pl.BlockSpec((s, hw), lambda bi, h, qi, pt: (bi, h)),
            pl.BlockSpec((s // tk, PV_ROWS, tk), lambda bi, h, qi, pt: (bi, h, 0)),
            pl.BlockSpec((1, DV_A), cst2),
            resident(QA_W), resident(KA_W), resident(VA_W),
            pl.BlockSpec(memory_space=pl.ANY),
            pl.BlockSpec(memory_space=pl.ANY),
        ],
        out_specs=[pl.BlockSpec((tq, DV_A), lambda bi, h, qi, pt: (bi * nq + qi, h)),
                   pl.BlockSpec((bd, t, VA_W), cst3)],
        scratch_shapes=[
            pltpu.VMEM((hw, 2 * tq), BF16), pltpu.VMEM((2, tk, 2 * tq), F32),
            pltpu.VMEM((2, tk, 2 * tq), BF16), pltpu.VMEM((PV_ROWS, 2 * tq), F32),
            pltpu.VMEM((2, pp * rows, 2 * DH_A), F32),
            pltpu.VMEM((2, pp * rows, DV_A), F32),
            pltpu.SemaphoreType.DMA((2, 2)),
            pltpu.VMEM((H_A * 2 * t, 1), F32),
            pltpu.VMEM((H_A * 2 * t, 1), F32),
            pltpu.VMEM((H_A * 2 * t, DV_A), F32),
        ],
    )
    return pl.pallas_call(
        functools.partial(_dattn_kernel, tq=tq, tk=tk, pp=pp, t=t, ns=ns, ng=ng, guard=ns < nsteps,
                          lam_init=lam_init),
        grid_spec=grid_spec,
        out_shape=[jax.ShapeDtypeStruct((n, VA_W), BF16), jax.ShapeDtypeStruct((bd, t, VA_W), F32)],
        compiler_params=_params("arbitrary", "arbitrary", "arbitrary"), name="dattn",
    )(page_table, *lams, g_head.reshape(DV_A, 1), qat, kab, vat, g_head.reshape(1, DV_A), qa_s, kn, vn, ck, cv)


def _ret_kernel(q_ref, k_ref, v_ref, g_ref, s0_ref, dmat_ref, qdec_ref, kdec_ref, sdec_ref,
                o_ref, sout_ref, s_scr, *, c, cps, mm):
    ci = pl.program_id(1)

    @pl.when(ci == 0)
    def _():
        s_scr[...] = s0_ref[0].reshape(s_scr.shape)

    head_of_lane = lax.broadcasted_iota(jnp.int32, (c, QR_W), 1) // DK_R
    for step in range(cps):
        r0 = step * c
        q = q_ref[0, r0:r0 + c, :]
        k = k_ref[0, r0:r0 + c, :]
        v = v_ref[0, r0:r0 + c, :].astype(mm)
        gate = g_ref[0, r0:r0 + c, :]
        s_old = s_scr[...]
        s_mm = s_old.astype(mm)
        k_mm = k.astype(mm)
        kd = (k.astype(F32) * kdec_ref[...]).astype(mm)
        upd = lax.dot_general(kd, v, _TN, preferred_element_type=F32)
        for h in range(H_R):
            qh = jnp.where(head_of_lane == h, q, jnp.zeros_like(q)).astype(mm)
            vh = v[:, h * DV_R:(h + 1) * DV_R]
            s = lax.dot_general(qh, k_mm, _NT, preferred_element_type=F32) * dmat_ref[h]
            intra = jnp.dot(s.astype(mm), vh, preferred_element_type=F32)
            cross = jnp.dot(qh, s_mm, preferred_element_type=F32) * qdec_ref[:, h * DV_R:(h + 1) * DV_R]
            gh = gate[:, h * DV_R:(h + 1) * DV_R]
            y = _rms(intra + cross) * (gh * jax.nn.sigmoid(gh))
            o_ref[0, r0:r0 + c, h * DV_R:(h + 1) * DV_R] = y.astype(o_ref.dtype)
        diag = [upd[h * DK_R:(h + 1) * DK_R, h * DV_R:(h + 1) * DV_R] for h in range(H_R)]
        s_scr[...] = s_old * sdec_ref[...] + jnp.concatenate(diag, axis=0)

    @pl.when(ci == pl.num_programs(1) - 1)
    def _():
        sout_ref[0] = s_scr[...].reshape(sout_ref.shape[1:])


def _retention(qr, kr, vr, gr, s0, tabs, *, c, cps, mm, out_dtype):
    b, t, _ = qr.shape
    dmat, qdec, kdec, sdec = tabs
    tt = c * cps
    blk = lambda bi, ci: (bi, ci, 0)
    st = lambda bi, ci: (bi, 0, 0, 0)
    return pl.pallas_call(
        functools.partial(_ret_kernel, c=c, cps=cps, mm=mm),
        grid=(b, t // tt),
        in_specs=[pl.BlockSpec((1, tt, QR_W), blk), pl.BlockSpec((1, tt, KR_W), blk),
                  pl.BlockSpec((1, tt, VR_W), blk), pl.BlockSpec((1, tt, GR_W), blk),
                  pl.BlockSpec((1, H_R, DK_R, DV_R), st),
                  _const_spec(dmat.shape), _const_spec(qdec.shape), _const_spec(kdec.shape),
                  _const_spec(sdec.shape)],
        out_specs=[pl.BlockSpec((1, tt, VR_W), blk), pl.BlockSpec((1, H_R, DK_R, DV_R), st)],
        out_shape=[jax.ShapeDtypeStruct((b, t, VR_W), out_dtype),
                   jax.ShapeDtypeStruct((b, H_R, DK_R, DV_R), F32)],
        scratch_shapes=[pltpu.VMEM((H_R * DK_R, DV_R), F32)],
        compiler_params=_params("parallel", "arbitrary"), name="retention",
    )(qr, kr, vr, gr, s0, dmat, qdec, kdec, sdec)


def _mix_kernel(oa_ref, or_ref, h_ref, wo_ref, gpost_ref, gpre_ref, wq_ref, h1_ref, qm_ref, *, dh_m):
    mix_in = jnp.concatenate([oa_ref[...], or_ref[...]], axis=-1).astype(BF16)
    mix = jnp.dot(mix_in, wo_ref[...], preferred_element_type=F32)
    h1 = h_ref[...] + _rms(mix, gpost_ref[...])
    h1_ref[...] = h1
    xn = _rms(h1, gpre_ref[...]).astype(BF16)
    qm_ref[...] = (jnp.dot(xn, wq_ref[...], preferred_element_type=F32) * dh_m ** -0.5).astype(qm_ref.dtype)


def _mix(oa, orr, h, w_out, g_post, g_pre, w_mq, *, tm, act_dtype):
    n, d = h.shape
    row = lambda i: (i, 0)
    return pl.pallas_call(
        functools.partial(_mix_kernel, dh_m=d // H_M),
        grid=(n // tm,),
        in_specs=[pl.BlockSpec((tm, VA_W), row), pl.BlockSpec((tm, VR_W), row), pl.BlockSpec((tm, d), row),
                  _const_spec(w_out.shape), _const_spec((1, d)), _const_spec((1, d)), _const_spec(w_mq.shape)],
        out_specs=[pl.BlockSpec((tm, d), row), pl.BlockSpec((tm, d), row)],
        out_shape=[jax.ShapeDtypeStruct((n, d), F32), jax.ShapeDtypeStruct((n, d), act_dtype)],
        compiler_params=_params("parallel"), name="mix_out",
    )(oa, orr, h, w_out, g_post, g_pre, w_mq)


def _mem_chunk(h, c):
    return c * H_M + h


def _memkv_kernel(mem_ref, g_ref, wk_ref, wv_ref, mk_ref, mv_ref):
    mn = _rms(mem_ref[...], g_ref[...]).astype(BF16)
    tm, d = mn.shape
    halves = d // H_M // LANES
    nchunk = H_M * halves
    for w_ref, o_ref in ((wk_ref, mk_ref), (wv_ref, mv_ref)):
        y = jnp.dot(mn, w_ref[...], preferred_element_type=F32)
        for h in range(H_M):
            for c in range(halves):
                col = (h * halves + c) * LANES
                o_ref[pl.ds(_mem_chunk(h, c), tm, stride=nchunk), :] = y[:, col:col + LANES]


def _memkv(mem2d, g, w_mk, w_mv, *, tm):
    n, d = mem2d.shape
    nchunk = d // LANES
    row = lambda i: (i, 0)
    return pl.pallas_call(
        _memkv_kernel, grid=(n // tm,),
        in_specs=[pl.BlockSpec((tm, d), row), _const_spec((1, d)), _const_spec(w_mk.shape), _const_spec(w_mv.shape)],
        out_specs=[pl.BlockSpec((tm * nchunk, LANES), row)] * 2,
        out_shape=[jax.ShapeDtypeStruct((n * nchunk, LANES), F32)] * 2,
        compiler_params=_params("parallel"), name="mem_kv",
    )(mem2d, g, w_mk, w_mv)


def _memattn_kernel(q_ref, mk_ref, mv_ref, o_ref, *, mm, nb):
    d = q_ref.shape[-1]
    dh = d // H_M
    halves = dh // LANES
    nchunk = H_M * halves
    n_mem = mk_ref.shape[1] // nchunk

    def head(ref, bi, h):
        parts = [ref[bi, pl.ds(_mem_chunk(h, c), n_mem, stride=nchunk), :] for c in range(halves)]
        return jnp.concatenate(parts, axis=-1).astype(mm)

    for bi in range(nb):
        q = q_ref[bi]
        for h in range(H_M):
            qh = q[:, h * dh:(h + 1) * dh].astype(mm)
            s = lax.dot_general(qh, head(mk_ref, bi, h), _NT, preferred_element_type=F32)
            p = jnp.exp(s - jnp.max(s, axis=-1, keepdims=True))
            p = p * (1.0 / jnp.sum(p, axis=-1, keepdims=True))
            o = jnp.dot(p.astype(mm), head(mv_ref, bi, h), preferred_element_type=F32)
            o_ref[bi, :, h * dh:(h + 1) * dh] = o.astype(o_ref.dtype)


def _memattn(qm, mk, mv, *, tq, nb, mm):
    b, t, d = qm.shape
    rows = mk.shape[1]
    return pl.pallas_call(
        functools.partial(_memattn_kernel, mm=mm, nb=nb),
        grid=(b // nb, t // tq),
        in_specs=[pl.BlockSpec((nb, tq, d), lambda bi, qi: (bi, qi, 0)),
                  pl.BlockSpec((nb, rows, LANES), lambda bi, qi: (bi, 0, 0)),
                  pl.BlockSpec((nb, rows, LANES), lambda bi, qi: (bi, 0, 0))],
        out_specs=pl.BlockSpec((nb, tq, d), lambda bi, qi: (bi, qi, 0)),
        out_shape=jax.ShapeDtypeStruct((b, t, d), qm.dtype),
        compiler_params=_params("parallel", "arbitrary"), name="mem_attn",
    )(qm, mk, mv)


def _ffn_kernel(h1_ref, o_ref, wmo_ref, gpm_ref, gpre_ref, gpost_ref, wg_ref, wu_ref, cw_ref, cb_ref, wd_ref,
                *rest, tm, seq, fw, carry):
    if carry:
        prev_ref, y_ref, cnew_ref, a_scr, h_scr = rest
    else:
        p1_ref, p2_ref, y_ref, a_ref, a_scr, h_scr = rest
    i = pl.program_id(0)
    pad = SUBLANES
    m = jnp.dot(o_ref[...].astype(BF16), wmo_ref[...], preferred_element_type=F32)
    h2 = h1_ref[...] + _rms(m, gpm_ref[...])
    xn = _rms(h2, gpre_ref[...]).astype(BF16)
    d_ff = wg_ref.shape[1]

    if carry:
        @pl.when(i % (seq // tm) == 0)
        def _():
            a_scr[pad - 2:pad, :] = prev_ref[0]
    else:
        a_scr[0:pad, :] = jnp.zeros((pad, d_ff), F32)
        tpos = lax.broadcasted_iota(jnp.int32, (tm, fw), 0) % seq

    for c0 in range(0, d_ff, fw):
        cs = slice(c0, c0 + fw)
        a = jnp.dot(xn, wg_ref[:, cs], preferred_element_type=F32)
        a_scr[pad:pad + tm, cs] = a
        s1 = a_scr[pad - 1:pad - 1 + tm, cs]
        s2 = a_scr[pad - 2:pad - 2 + tm, cs]
        if not carry:
            s1 = jnp.where(tpos == 0, p1_ref[:, cs], s1)
            s2 = jnp.where(tpos < 2, p2_ref[:, cs], s2)
        conv = cb_ref[:, cs] + s2 * cw_ref[0:1, cs] + s1 * cw_ref[1:2, cs] + a * cw_ref[2:3, cs]
        up = jnp.dot(xn, wu_ref[:, cs], preferred_element_type=F32)
        h_scr[:, cs] = (_gelu_exact(conv) * up).astype(BF16)

    if carry:
        cnew_ref[0] = a_scr[pad + tm - 2:pad + tm, :]
        a_scr[0:pad, :] = a_scr[tm:tm + pad, :]
    else:
        a_ref[...] = a_scr[pad:pad + tm, :]
    f = jnp.dot(h_scr[...], wd_ref[...], preferred_element_type=F32)
    y_ref[...] = h2 + _rms(f, gpost_ref[...])


def _ffn(h1, o, w_mo, g_post_mem, g_pre, g_post, w_gate, w_up, conv_w, conv_b, w_down, prev, *, tm, seq, fw, carry):
    n, d = h1.shape
    d_ff = w_gate.shape[1]
    row = lambda i: (i, 0)
    in_specs = [pl.BlockSpec((tm, d), row), pl.BlockSpec((tm, d), row), _const_spec(w_mo.shape),
                _const_spec((1, d)), _const_spec((1, d)), _const_spec((1, d)),
                _const_spec(w_gate.shape), _const_spec(w_up.shape), _const_spec(conv_w.shape),
                _const_spec((1, d_ff)), _const_spec(w_down.shape)]
    out_specs = [pl.BlockSpec((tm, d), row)]
    out_shape = [jax.ShapeDtypeStruct((n, d), F32)]
    if carry:
        tps = seq // tm
        in_specs += [pl.BlockSpec((1, CONV_W - 1, d_ff), lambda i: (i // tps, 0, 0))]
        out_specs += [pl.BlockSpec((1, CONV_W - 1, d_ff), lambda i: (i // tps, 0, 0))]
        out_shape += [jax.ShapeDtypeStruct((n // seq, CONV_W - 1, d_ff), F32)]
        extra = (prev,)
    else:
        in_specs += [pl.BlockSpec((tm, d_ff), row)] * 2
        out_specs += [pl.BlockSpec((tm, d_ff), row)]
        out_shape += [jax.ShapeDtypeStruct((n, d_ff), F32)]
        extra = prev
    return pl.pallas_call(
        functools.partial(_ffn_kernel, tm=tm, seq=seq, fw=fw, carry=carry),
        grid=(n // tm,), in_specs=in_specs, out_specs=out_specs, out_shape=out_shape,
        scratch_shapes=[pltpu.VMEM((tm + SUBLANES, d_ff), F32), pltpu.VMEM((tm, d_ff), BF16)],
        compiler_params=_params("arbitrary"), name="mem_out_ffn",
    )(h1, o, w_mo, g_post_mem, g_pre, g_post, w_gate, w_up, conv_w, conv_b, w_down, *extra)


def _rotary_tables(pos, k_scale):
    half = DK_R // 2
    inv = 1.0 / (ROPE_BASE ** jnp.linspace(0.0, 1.0, half, dtype=F32))
    ang = pos.astype(F32)[:, None] * inv[None, :]
    cos, sin = jnp.cos(ang), jnp.sin(ang)
    cos_t = jnp.tile(jnp.concatenate([cos, cos], axis=-1), (1, H_R))
    sin_t = jnp.tile(jnp.concatenate([-sin, sin], axis=-1), (1, H_R))
    return cos_t, sin_t, cos_t * k_scale, sin_t * k_scale


def _retention_tables(c):
    log_g = jnp.log(1.0 - 2.0 ** (-5.0 - jnp.arange(H_R, dtype=F32)))
    i = jnp.arange(c, dtype=F32)
    diff = i[:, None] - i[None, :]
    dmat = jnp.where(diff[None] >= 0, jnp.exp(jnp.maximum(diff, 0.0)[None] * log_g[:, None, None]), 0.0)
    qdec = jnp.exp((i + 1.0)[:, None] * log_g[None, :])
    kdec = jnp.exp((c - 1.0 - i)[:, None] * log_g[None, :])
    sdec = jnp.exp(c * log_g)
    return (dmat, jnp.repeat(qdec, DV_R, axis=1), jnp.repeat(kdec, DK_R, axis=1),
            jnp.broadcast_to(jnp.repeat(sdec, DK_R)[:, None], (H_R * DK_R, DV_R)))


def _pick(n, pref):
    t = min(n, pref)
    while n % t:
        t //= 2
    return t


def kernel(x_prompt, x_sample, cache_k, cache_v, cache_mem_k, cache_mem_v, state_ret, state_conv, page_table, mem_prompt, w_in, w_out, lam_q1, lam_k1, lam_q2, lam_k2, g_diff_head, g_pre_mix, g_post_mix, g_pre_mem, g_post_mem, g_mem_in, w_mq, w_mk, w_mv, w_mo, g_pre_ffn, g_post_ffn, w_gate, w_up, conv_w, conv_b, w_down):
    b, s, d = x_prompt.shape
    bd, t, _ = x_sample.shape
    depth = w_in.shape[0]
    n_mem = mem_prompt.shape[1]
    d_ff = w_gate.shape[-1]
    past = page_table.shape[1] * PAGE_SIZE
    n_pool = cache_k.shape[1]
    assert depth == 1, "the sample path keeps the layer loop unrolled for a single layer"
    l = 0
    lam_init = 0.8 - 0.6 * math.exp(-0.3 * l)

    row = lambda v: v[l].reshape(1, -1)
    bf = lambda w: w[l].astype(BF16)
    w_in_b, w_out_b, w_mq_b, w_mk_b, w_mv_b, w_mo_b = map(bf, (w_in, w_out, w_mq, w_mk, w_mv, w_mo))
    w_gate_b, w_up_b, w_down_b = map(bf, (w_gate, w_up, w_down))
    lams = tuple(row(v) for v in (lam_q1, lam_k1, lam_q2, lam_k2))
    g_head = g_diff_head[l]

    tm_p = _pick(s, 512)
    tkv = _pick(s, 256)
    tq = _pick(s, 512)
    ret_c = RET_CHUNK if s % RET_CHUNK == 0 else s
    ret_c_s = RET_CHUNK if t % RET_CHUNK == 0 else t
    n_s = bd * t
    dh_m = d // H_M
    halves = dh_m // LANES
    nchunk = H_M * halves

    def mem_rows(a):
        g = a.shape[0]
        return a.reshape(g, n_mem, H_M, halves, LANES).transpose(0, 1, 3, 2, 4).reshape(g, n_mem * nchunk, LANES)

    def mem_heads(a):
        g = a.shape[0]
        return a.reshape(g, n_mem, halves, H_M, LANES).transpose(0, 1, 3, 2, 4).reshape(g, n_mem, H_M, dh_m)

    mk_p, mv_p = _memkv(mem_prompt.reshape(b * n_mem, d), row(g_mem_in), w_mk_b, w_mv_b, tm=_pick(b * n_mem, 512))
    mk_p = mk_p.reshape(b, n_mem * nchunk, LANES)
    mv_p = mv_p.reshape(b, n_mem * nchunk, LANES)
    tabs_p = _rotary_tables(jnp.arange(s), DK_R ** -0.5)
    qat, ka, va, kab, vat, qr, kr, vr, gr = _inproj(
        x_prompt.reshape(b * s, d), row(g_pre_mix), w_in_b, tabs_p, tm=tm_p, prompt=True, tkv=tkv, tqb=tq,
        q_scale=DH_A ** -0.5 * math.log2(math.e), act_dtype=BF16)
    tm_s = _pick(n_s, 512)
    reps = tm_s // t
    tabs_s = tuple(jnp.tile(a, (reps, 1)) for a in _rotary_tables(past + jnp.arange(t), DK_R ** -0.5))
    qa_s, ka_s, va_s, qr_s, kr_s, vr_s, gr_s = _inproj(
        x_sample.reshape(n_s, d), row(g_pre_mix), w_in_b, tabs_s, tm=tm_s, prompt=False, tkv=tkv, tqb=tq,
        q_scale=DH_A ** -0.5, act_dtype=F32)
    b3 = lambda a: a.reshape(bd, t, a.shape[-1])
    oa, oa_s = _dattn(page_table, lams, g_head, qat, kab, vat, b3(qa_s), b3(ka_s), b3(va_s),
                      cache_k[l].reshape(n_pool, PAGE_SIZE * H_A, 2 * DH_A),
                      cache_v[l].reshape(n_pool, PAGE_SIZE * H_A, DV_A),
                      b=b, s=s, tq=tq, tk=tkv, pp=_pick(page_table.shape[1], 16), lam_init=lam_init)
    r3 = lambda a: a.reshape(b, s, a.shape[-1])
    orr, ret_p = _retention(r3(qr), r3(kr), r3(vr), r3(gr), jnp.zeros((b, H_R, DK_R, DV_R), F32),
                            _retention_tables(ret_c), c=ret_c, cps=_pick(s // ret_c, 4), mm=BF16, out_dtype=BF16)
    h1, qm = _mix(oa, orr.reshape(b * s, VR_W), x_prompt.reshape(b * s, d), w_out_b, row(g_post_mix),
                  row(g_pre_mem), w_mq_b, tm=tm_p, act_dtype=BF16)
    om = _memattn(qm.reshape(b, s, d), mk_p, mv_p, tq=tm_p, nb=1, mm=BF16)
    tm_f = _pick(s, 256)
    y_p, conv_p = _ffn(h1, om.reshape(b * s, d), w_mo_b, row(g_post_mem), row(g_pre_ffn), row(g_post_ffn),
                       w_gate_b, w_up_b, conv_w[l], row(conv_b), w_down_b,
                       jnp.zeros((b, CONV_W - 1, d_ff), F32), tm=tm_f, seq=s, fw=d_ff // 2, carry=True)

    orr_s, ret_s = _retention(b3(qr_s), b3(kr_s), b3(vr_s), b3(gr_s), state_ret[l], _retention_tables(ret_c_s),
                              c=ret_c_s, cps=_pick(t // ret_c_s, 4), mm=F32, out_dtype=F32)
    h1_s, qm_s = _mix(oa_s.reshape(n_s, VA_W), orr_s.reshape(n_s, VR_W), x_sample.reshape(n_s, d), w_out_b,
                      row(g_post_mix), row(g_pre_mem), w_mq_b, tm=tm_s, act_dtype=F32)
    om_s = _memattn(b3(qm_s), mem_rows(cache_mem_k[l]), mem_rows(cache_mem_v[l]), tq=t, nb=_pick(bd, 4), mm=F32)
    prev = state_conv[l]
    zeros = jnp.zeros((bd, t - 2, d_ff), F32)
    p1 = jnp.concatenate([prev[:, 1:2], jnp.zeros((bd, 1, d_ff), F32), zeros], axis=1).reshape(n_s, d_ff)
    p2 = jnp.concatenate([prev, zeros], axis=1).reshape(n_s, d_ff)
    tm_fs = _pick(n_s, 256)
    y_s, a_s = _ffn(h1_s, om_s.reshape(n_s, d), w_mo_b, row(g_post_mem), row(g_pre_ffn), row(g_post_ffn),
                    w_gate_b, w_up_b, conv_w[l], row(conv_b), w_down_b, (p1, p2),
                    tm=tm_fs, seq=t, fw=d_ff // 2, carry=False)
    conv_s = a_s.reshape(bd, t, d_ff)[:, t - (CONV_W - 1):]

    return (y_p.reshape(b, s, d), y_s.reshape(bd, t, d),
            ka.reshape(1, b, s, H_A, 2 * DH_A), va.reshape(1, b, s, H_A, DV_A),
            mem_heads(mk_p)[None], mem_heads(mv_p)[None],
            ret_p[None], conv_p[None],
            ka_s.reshape(1, bd, t, H_A, 2 * DH_A), va_s.reshape(1, bd, t, H_A, DV_A),
            ret_s[None], conv_s[None])
```

```python
import functools
import math

import jax
import jax.numpy as jnp
from jax import lax
from jax.experimental import pallas as pl
from jax.experimental.pallas import tpu as pltpu

F32 = jnp.float32
BF16 = jnp.bfloat16

H_A = 4
DH_A = 64
DV_A = 2 * DH_A
H_R = 4
DK_R = 64
DV_R = 128
RET_CHUNK = 128
ROPE_BASE = 10000.0
H_M = 4
CONV_W = 3
PAGE_SIZE = 128
EPS = 1e-6
NEG = -1e30

QA_W = H_A * 2 * DH_A
KA_W = H_A * 2 * DH_A
VA_W = H_A * DV_A
QR_W = H_R * DK_R
KR_W = H_R * DK_R
VR_W = H_R * DV_R
GR_W = H_R * DV_R

VMEM_LIMIT = 56 * 1024 * 1024
SUBLANES = 8
LANES = 128
SUM_ROWS = 16
PV_ROWS = DV_A + SUM_ROWS

_NT = (((1,), (1,)), ((), ()))
_TN = (((0,), (0,)), ((), ()))


def _params(*sem):
    return pltpu.CompilerParams(dimension_semantics=sem, vmem_limit_bytes=VMEM_LIMIT)


def _const_spec(shape):
    nd = len(shape)
    return pl.BlockSpec(shape, lambda *_: (0,) * nd, pipeline_mode=pl.Buffered(1))


def _rms(x, g=None):
    y = x * lax.rsqrt(jnp.mean(x * x, axis=-1, keepdims=True) + EPS)
    return y if g is None else y * g


def _gelu_exact(x):
    return 0.5 * x * (1.0 + lax.erf(x * (2.0 ** -0.5)))


def _diff_lambda(lq1, lk1, lq2, lk2, lam_init):
    a = jnp.sum(lq1[...] * lk1[...], axis=-1, keepdims=True)
    b = jnp.sum(lq2[...] * lk2[...], axis=-1, keepdims=True)
    return jnp.exp(a) - jnp.exp(b) + lam_init


def _rotate(x, cos_t, sin_t):
    half = DK_R // 2
    w = x.shape[-1]
    lane = lax.broadcasted_iota(jnp.int32, x.shape, 1)
    swapped = jnp.where(lane % DK_R < half, pltpu.roll(x, w - half, 1), pltpu.roll(x, half, 1))
    return x * cos_t + swapped * sin_t


def _inproj_kernel(x_ref, g_ref, w_ref, cq_ref, sq_ref, ck_ref, sk_ref, *out_refs, prompt, tkv, q_scale):
    if prompt:
        qa_ref, ka_ref, va_ref, kab_ref, vat_ref, qr_ref, kr_ref, vr_ref, gr_ref = out_refs
    else:
        qa_ref, ka_ref, va_ref, qr_ref, kr_ref, vr_ref, gr_ref = out_refs
    xn = _rms(x_ref[...], g_ref[...]).astype(BF16)
    tm = xn.shape[0]

    def proj(c0, width):
        return jnp.dot(xn, w_ref[:, c0:c0 + width], preferred_element_type=F32)

    def store_heads(ref, val, width):
        if prompt:
            for h in range(H_A):
                ref[pl.ds(h, tm, stride=H_A), :] = val[:, h * width:(h + 1) * width]
        else:
            ref[...] = val

    c = 0
    qa = proj(c, QA_W) * q_scale
    if prompt:
        tqb = qa_ref.shape[-1]
        for r in range(tm // tqb):
            qa_ref[r] = qa[r * tqb:(r + 1) * tqb, :].T.astype(qa_ref.dtype)
    else:
        qa_ref[...] = qa.astype(qa_ref.dtype)
    c += QA_W
    ka = proj(c, KA_W)
    store_heads(ka_ref, ka, 2 * DH_A)
    c += KA_W
    va = proj(c, VA_W)
    store_heads(va_ref, va, DV_A)
    c += VA_W
    if prompt:
        kab_ref[...] = ka.astype(BF16)
        ones = jnp.ones((SUM_ROWS, tkv), BF16)
        for r in range(tm // tkv):
            vt = va[r * tkv:(r + 1) * tkv, :].T.astype(BF16)
            for h in range(H_A):
                vat_ref[r, h * PV_ROWS:h * PV_ROWS + DV_A, :] = vt[h * DV_A:(h + 1) * DV_A, :]
                vat_ref[r, h * PV_ROWS + DV_A:(h + 1) * PV_ROWS, :] = ones
    qr_ref[...] = _rotate(proj(c, QR_W), cq_ref[...], sq_ref[...]).astype(qr_ref.dtype)
    c += QR_W
    kr_ref[...] = _rotate(proj(c, KR_W), ck_ref[...], sk_ref[...]).astype(kr_ref.dtype)
    c += KR_W
    vr_ref[...] = proj(c, VR_W).astype(vr_ref.dtype)
    c += VR_W
    gr_ref[...] = proj(c, GR_W)


def _inproj(x2d, g, w_in, tabs, *, tm, prompt, tkv, tqb, q_scale, act_dtype):
    n, d = x2d.shape
    period = tabs[0].shape[0]
    npb = period // tm
    row = lambda i: (i, 0)
    tab = lambda i: (i % npb, 0)
    in_specs = [pl.BlockSpec((tm, d), row), _const_spec((1, d)), _const_spec(w_in.shape)]
    in_specs += [pl.BlockSpec((tm, QR_W), tab)] * 4
    if prompt:
        out_shape = [jax.ShapeDtypeStruct((n // tqb, QA_W, tqb), act_dtype)]
        out_specs = [pl.BlockSpec((tm // tqb, QA_W, tqb), lambda i: (i, 0, 0))]
        out_shape += [jax.ShapeDtypeStruct((n * H_A, 2 * DH_A), F32),
                      jax.ShapeDtypeStruct((n * H_A, DV_A), F32),
                      jax.ShapeDtypeStruct((n, KA_W), BF16),
                      jax.ShapeDtypeStruct((n // tkv, H_A * PV_ROWS, tkv), BF16)]
        out_specs += [pl.BlockSpec((tm * H_A, 2 * DH_A), row), pl.BlockSpec((tm * H_A, DV_A), row),
                      pl.BlockSpec((tm, KA_W), row),
                      pl.BlockSpec((tm // tkv, H_A * PV_ROWS, tkv), lambda i: (i, 0, 0))]
    else:
        out_shape = [jax.ShapeDtypeStruct((n, QA_W), act_dtype),
                     jax.ShapeDtypeStruct((n, KA_W), F32), jax.ShapeDtypeStruct((n, VA_W), F32)]
        out_specs = [pl.BlockSpec((tm, QA_W), row), pl.BlockSpec((tm, KA_W), row), pl.BlockSpec((tm, VA_W), row)]
    out_shape += [jax.ShapeDtypeStruct((n, QR_W), act_dtype),
                  jax.ShapeDtypeStruct((n, KR_W), act_dtype),
                  jax.ShapeDtypeStruct((n, VR_W), act_dtype),
                  jax.ShapeDtypeStruct((n, GR_W), F32)]
    out_specs += [pl.BlockSpec((tm, QR_W), row), pl.BlockSpec((tm, KR_W), row),
                  pl.BlockSpec((tm, VR_W), row), pl.BlockSpec((tm, GR_W), row)]
    return pl.pallas_call(
        functools.partial(_inproj_kernel, prompt=prompt, tkv=tkv, q_scale=q_scale),
        grid=(n // tm,), in_specs=in_specs, out_specs=out_specs, out_shape=out_shape,
        compiler_params=_params("parallel"), name="inproj",
    )(x2d, g, w_in, *tabs)


def _dattn_p_body(lq1, lk1, lq2, lk2, g_ref, qt_ref, k_ref, vt_ref, o_ref, qbd_scr, st_scr, p_scr, acc_scr,
                  *, tq, tk, lam_init, after_loop):
    qi = pl.program_id(2)
    nfull = 2 * qi
    qt = qt_ref[0]
    row = lax.broadcasted_iota(jnp.int32, qt.shape, 0)
    zero = jnp.zeros_like(qt)
    qbd_scr[:, 0:tq] = jnp.where(row < DH_A, qt, zero)
    qbd_scr[:, tq:2 * tq] = jnp.where(row >= DH_A, qt, zero)
    acc_scr[...] = jnp.zeros(acc_scr.shape, F32)
    p_scr[1] = jnp.zeros(p_scr.shape[1:], BF16)

    gw = 2 * LANES
    ngroup = 2 * tq // gw

    def tile_class(d, c):
        q0 = (c * LANES) % tq
        if q0 + LANES - 1 < d * tk:
            return "skip"
        if q0 >= d * tk + tk - 1:
            return "full"
        return "mask"

    def group_live(d, g):
        return any(tile_class(d, c) != "skip" for c in range(g * gw // LANES, (g + 1) * gw // LANES))

    def scores(j, slot, g):
        kj = k_ref[pl.ds(pl.multiple_of(j * tk, tk), tk), :]
        st_scr[slot, :, g * gw:(g + 1) * gw] = jnp.dot(
            kj, qbd_scr[:, g * gw:(g + 1) * gw], preferred_element_type=F32)

    def block(j, slot, m, diag):
        vt_prev = vt_ref[jnp.maximum(j - 1, 0)]
        m_out = []
        for g in range(ngroup):
            gs = slice(g * gw, (g + 1) * gw)
            if diag is None or (diag == 0 and group_live(1, g)):
                scores(j + 1, 1 - slot, g)
            acc = acc_scr[:, gs]
            if diag != 1 or group_live(0, g):
                acc = acc + jnp.dot(vt_prev, p_scr[1 - slot, :, gs], preferred_element_type=F32)
            alphas = []
            for c in range(g * gw // LANES, (g + 1) * gw // LANES):
                cs = slice(c * LANES, (c + 1) * LANES)
                cls = "full" if diag is None else tile_class(diag, c)
                if cls == "skip":
                    m_out.append(m[:, cs])
                    continue
                st = st_scr[slot, :, cs]
                if cls == "mask":
                    kpos = diag * tk + lax.broadcasted_iota(jnp.int32, st.shape, 0)
                    qpos = (c * LANES) % tq + lax.broadcasted_iota(jnp.int32, st.shape, 1)
                    st = jnp.where(kpos <= qpos, st, NEG)
                m_new = jnp.maximum(m[:, cs], jnp.max(st, axis=0, keepdims=True))
                alphas.append(jnp.exp2(m[:, cs] - m_new))
                m_out.append(m_new)
                p_scr[slot, :, cs] = jnp.exp2((st - m_new).astype(BF16))
            assert len(alphas) in (0, gw // LANES), "a lane group is skipped or processed as a whole"
            acc_scr[:, gs] = acc * jnp.concatenate(alphas, axis=-1) if alphas else acc
        return jnp.concatenate(m_out, axis=-1)

    def pair(i, m):
        return block(2 * i + 1, 1, block(2 * i, 0, m, None), None)

    for g in range(ngroup):
        scores(0, 0, g)
    m = lax.fori_loop(0, qi // 2, lambda i, m: pair(2 * i + 1, pair(2 * i, m)), jnp.full((1, 2 * tq), NEG, F32))
    m = lax.fori_loop(qi - qi % 2, qi, pair, m)
    after_loop()
    m = block(nfull, 0, m, 0)
    m = block(nfull + 1, 1, m, 1)
    vt_last = vt_ref[nfull + 1]
    for g in range(ngroup):
        if group_live(1, g):
            gs = slice(g * gw, (g + 1) * gw)
            acc_scr[:, gs] = acc_scr[:, gs] + jnp.dot(vt_last, p_scr[1, :, gs], preferred_element_type=F32)
    lam = _diff_lambda(lq1, lk1, lq2, lk2, lam_init)
    o = acc_scr[0:DV_A, :] * (1.0 / acc_scr[DV_A:DV_A + 1, :])
    o = o[:, :tq] - lam * o[:, tq:]
    y = o * lax.rsqrt(jnp.mean(o * o, axis=0, keepdims=True) + EPS) * g_ref[...] * (1.0 - lam_init)
    o_ref[...] = y.T.astype(o_ref.dtype)


def _decode_hooks(pt_ref, lams, g_ref, q_ref, kn_ref, vn_ref, ck_hbm, cv_hbm, o_ref,
                  kbuf, vbuf, sems, m_scr, l_scr, acc_scr, *, sidx, ns, ng, pp, t, lam_init, guard):
    slot = sidx % 2
    bi = sidx // ng
    gi = sidx % ng
    rows = PAGE_SIZE * H_A
    hm = 2 * t
    nkeys = pp * PAGE_SIZE

    def copies(step_, slot_):
        b_ = step_ // ng
        g_ = step_ % ng
        out = []
        for p in range(pp):
            page = pt_ref[b_, g_ * pp + p]
            out.append(pltpu.make_async_copy(ck_hbm.at[page], kbuf.at[slot_, pl.ds(p * rows, rows)], sems.at[0, slot_]))
            out.append(pltpu.make_async_copy(cv_hbm.at[page], vbuf.at[slot_, pl.ds(p * rows, rows)], sems.at[1, slot_]))
        return out

    def prologue():
        @pl.when(sidx == 0)
        def _():
            for c in copies(sidx, slot):
                c.start()

        @pl.when(sidx + 1 < ns)
        def _():
            for c in copies(sidx + 1, 1 - slot):
                c.start()

    def queries():
        q = q_ref[bi]
        qbd = []
        for h in range(H_A):
            qh = q[:, h * 2 * DH_A:(h + 1) * 2 * DH_A]
            lane = lax.broadcasted_iota(jnp.int32, qh.shape, 1)
            qbd.append(jnp.concatenate([jnp.where(lane < DH_A, qh, 0.0), jnp.where(lane >= DH_A, qh, 0.0)], axis=0))
        return qbd

    def update(s_all, v_heads, fresh):
        m = jnp.where(fresh, NEG, m_scr[...])
        l = jnp.where(fresh, 0.0, l_scr[...])
        acc = jnp.where(fresh, 0.0, acc_scr[...])
        m_new = jnp.maximum(m, jnp.max(s_all, axis=-1, keepdims=True))
        alpha = jnp.exp(m - m_new)
        p = jnp.exp(s_all - m_new)
        l_scr[...] = alpha * l + jnp.sum(p, axis=-1, keepdims=True)
        m_scr[...] = m_new
        pv = [jnp.dot(p[h * hm:(h + 1) * hm].astype(v_heads[h].dtype), v_heads[h], preferred_element_type=F32)
              for h in range(H_A)]
        acc_scr[...] = acc * alpha + jnp.concatenate(pv, axis=0)

    def main_body():
        for c in copies(sidx, slot):
            c.wait()
        qbd = queries()
        s_heads, v_heads = [], []
        for h in range(H_A):
            kh = kbuf[slot, pl.ds(h, nkeys, stride=H_A), :].astype(BF16)
            v_heads.append(vbuf[slot, pl.ds(h, nkeys, stride=H_A), :].astype(BF16))
            s_heads.append(lax.dot_general(qbd[h].astype(BF16), kh, _NT, preferred_element_type=F32))
        update(jnp.concatenate(s_heads, axis=0), v_heads, gi == 0)

    def main():
        if guard:
            pl.when(sidx < ns)(main_body)
        else:
            main_body()

    def epilogue_body():
        qbd = queries()
        kn = kn_ref[bi]
        vn = vn_ref[bi]
        s_heads, v_heads = [], []
        for h in range(H_A):
            s_heads.append(lax.dot_general(qbd[h], kn[:, h * 2 * DH_A:(h + 1) * 2 * DH_A], _NT,
                                           preferred_element_type=F32))
            v_heads.append(vn[:, h * DV_A:(h + 1) * DV_A])
        s_new = jnp.concatenate(s_heads, axis=0)
        kidx = lax.broadcasted_iota(jnp.int32, s_new.shape, 1)
        qidx = lax.broadcasted_iota(jnp.int32, s_new.shape, 0) % t
        update(jnp.where(kidx <= qidx, s_new, NEG), v_heads, False)

        lam = _diff_lambda(*lams, lam_init)
        o = acc_scr[...] * (1.0 / l_scr[...])
        for h in range(H_A):
            oh = o[h * hm:h * hm + t] - lam * o[h * hm + t:(h + 1) * hm]
            y = _rms(oh, g_ref[...]) * (1.0 - lam_init)
            o_ref[bi, :, h * DV_A:(h + 1) * DV_A] = y.astype(o_ref.dtype)

    def epilogue():
        last = gi == ng - 1
        pl.when(jnp.logical_and(last, sidx < ns) if guard else last)(epilogue_body)

    return prologue, main, epilogue


def _dattn_kernel(pt_ref, lq1, lk1, lq2, lk2, gcol_ref, qt_ref, k_ref, vt_ref, grow_ref, qs_ref, kn_ref, vn_ref,
                  ck_hbm, cv_hbm, o_ref, os_ref, qbd_scr, st_scr, p_scr, acc_scr, kbuf, vbuf, sems, m_scr, l_scr,
                  accs_scr, *, tq, tk, pp, t, ns, ng, guard, lam_init):
    step = (pl.program_id(0) * pl.num_programs(1) + pl.program_id(1)) * pl.num_programs(2) + pl.program_id(2)
    lams = (lq1, lk1, lq2, lk2)
    prologue, decode, epilogue = _decode_hooks(
        pt_ref, lams, grow_ref, qs_ref, kn_ref, vn_ref, ck_hbm, cv_hbm, os_ref, kbuf, vbuf, sems, m_scr, l_scr,
        accs_scr, sidx=step, ns=ns, ng=ng, pp=pp, t=t, lam_init=lam_init, guard=guard)
    prologue()
    _dattn_p_body(*lams, gcol_ref, qt_ref, k_ref, vt_ref, o_ref, qbd_scr, st_scr, p_scr, acc_scr,
                  tq=tq, tk=tk, lam_init=lam_init, after_loop=decode)
    epilogue()


def _dattn(page_table, lams, g_head, qat, kab, vat, qa_s, kn, vn, ck, cv, *, b, s, tq, tk, pp, lam_init):
    n = kab.shape[0]
    hw = 2 * DH_A
    nq = s // tq
    assert tq == 2 * tk and s % tq == 0, "the kernel walks key blocks in pairs"
    assert qat.shape == (n // tq, QA_W, tq)
    bd, t, _ = qa_s.shape
    ng = page_table.shape[1] // pp
    ns = bd * ng
    nsteps = b * H_A * nq
    assert ns <= nsteps, "one decode step per prompt grid step"
    rows = PAGE_SIZE * H_A
    cst2 = lambda bi, h, qi, pt: (0, 0)
    cst3 = lambda bi, h, qi, pt: (0, 0, 0)
    resident = lambda w: pl.BlockSpec((bd, t, w), cst3, pipeline_mode=pl.Buffered(1))
    grid_spec = pltpu.PrefetchScalarGridSpec(
        num_scalar_prefetch=1,
        grid=(b, H_A, nq),
        in_specs=[pl.BlockSpec((1, DH_A), cst2)] * 4 + [
            pl.BlockSpec((DV_A, 1), cst2),
            pl.BlockSpec((1, hw, tq), lambda bi, h, qi, pt: (bi * nq + qi, h, 0)),
            pl.BlockSpec((s, hw), lambda bi, h, qi, pt: (bi, h)),
            pl.BlockSpec((s // tk, PV_ROWS, tk), lambda bi, h, qi, pt: (bi, h, 0)),
            pl.BlockSpec((1, DV_A), cst2),
            resident(QA_W), resident(KA_W), resident(VA_W),
            pl.BlockSpec(memory_space=pl.ANY),
            pl.BlockSpec(memory_space=pl.ANY),
        ],
        out_specs=[pl.BlockSpec((tq, DV_A), lambda bi, h, qi, pt: (bi * nq + qi, h)),
                   pl.BlockSpec((bd, t, VA_W), cst3)],
        scratch_shapes=[
            pltpu.VMEM((hw, 2 * tq), BF16), pltpu.VMEM((2, tk, 2 * tq), F32),
            pltpu.VMEM((2, tk, 2 * tq), BF16), pltpu.VMEM((PV_ROWS, 2 * tq), F32),
            pltpu.VMEM((2, pp * rows, 2 * DH_A), F32),
            pltpu.VMEM((2, pp * rows, DV_A), F32),
            pltpu.SemaphoreType.DMA((2, 2)),
            pltpu.VMEM((H_A * 2 * t, 1), F32),
            pltpu.VMEM((H_A * 2 * t, 1), F32),
            pltpu.VMEM((H_A * 2 * t, DV_A), F32),
        ],
    )
    return pl.pallas_call(
        functools.partial(_dattn_kernel, tq=tq, tk=tk, pp=pp, t=t, ns=ns, ng=ng, guard=ns < nsteps,
                          lam_init=lam_init),
        grid_spec=grid_spec,
        out_shape=[jax.ShapeDtypeStruct((n, VA_W), BF16), jax.ShapeDtypeStruct((bd, t, VA_W), F32)],
        compiler_params=_params("arbitrary", "arbitrary", "arbitrary"), name="dattn",
    )(page_table, *lams, g_head.reshape(DV_A, 1), qat, kab, vat, g_head.reshape(1, DV_A), qa_s, kn, vn, ck, cv)


def _ret_kernel(q_ref, k_ref, v_ref, g_ref, s0_ref, dmat_ref, qdec_ref, kdec_ref, sdec_ref,
                o_ref, sout_ref, s_scr, *, c, cps, nb, mm):
    for bi in range(nb):
        _ret_sequence(q_ref, k_ref, v_ref, g_ref, s0_ref, dmat_ref, qdec_ref, kdec_ref, sdec_ref,
                      o_ref, sout_ref, s_scr, bi=bi, c=c, cps=cps, mm=mm)


def _ret_sequence(q_ref, k_ref, v_ref, g_ref, s0_ref, dmat_ref, qdec_ref, kdec_ref, sdec_ref,
                  o_ref, sout_ref, s_scr, *, bi, c, cps, mm):
    ci = pl.program_id(1)

    @pl.when(ci == 0)
    def _():
        s_scr[bi] = s0_ref[bi].reshape(s_scr.shape[1:])

    head_of_lane = lax.broadcasted_iota(jnp.int32, (c, QR_W), 1) // DK_R
    for step in range(cps):
        r0 = step * c
        q = q_ref[bi, r0:r0 + c, :]
        k = k_ref[bi, r0:r0 + c, :]
        v = v_ref[bi, r0:r0 + c, :].astype(mm)
        gate = g_ref[bi, r0:r0 + c, :]
        s_old = s_scr[bi]
        s_mm = s_old.astype(mm)
        k_mm = k.astype(mm)
        kd = (k.astype(F32) * kdec_ref[...]).astype(mm)
        upd = lax.dot_general(kd, v, _TN, preferred_element_type=F32)
        for h in range(H_R):
            qh = jnp.where(head_of_lane == h, q, jnp.zeros_like(q)).astype(mm)
            vh = v[:, h * DV_R:(h + 1) * DV_R]
            s = lax.dot_general(qh, k_mm, _NT, preferred_element_type=F32) * dmat_ref[h]
            intra = jnp.dot(s.astype(mm), vh, preferred_element_type=F32)
            cross = jnp.dot(qh, s_mm, preferred_element_type=F32) * qdec_ref[:, h * DV_R:(h + 1) * DV_R]
            gh = gate[:, h * DV_R:(h + 1) * DV_R]
            y = _rms(intra + cross) * (gh * jax.nn.sigmoid(gh))
            o_ref[bi, r0:r0 + c, h * DV_R:(h + 1) * DV_R] = y.astype(o_ref.dtype)
        diag = [upd[h * DK_R:(h + 1) * DK_R, h * DV_R:(h + 1) * DV_R] for h in range(H_R)]
        s_scr[bi] = s_old * sdec_ref[...] + jnp.concatenate(diag, axis=0)

    @pl.when(ci == pl.num_programs(1) - 1)
    def _():
        sout_ref[bi] = s_scr[bi].reshape(sout_ref.shape[1:])


def _retention(qr, kr, vr, gr, s0, tabs, *, c, cps, nb, mm, out_dtype):
    b, t, _ = qr.shape
    dmat, qdec, kdec, sdec = tabs
    tt = c * cps
    blk = lambda bi, ci: (bi, ci, 0)
    st = lambda bi, ci: (bi, 0, 0, 0)
    return pl.pallas_call(
        functools.partial(_ret_kernel, c=c, cps=cps, nb=nb, mm=mm),
        grid=(b // nb, t // tt),
        in_specs=[pl.BlockSpec((nb, tt, QR_W), blk), pl.BlockSpec((nb, tt, KR_W), blk),
                  pl.BlockSpec((nb, tt, VR_W), blk), pl.BlockSpec((nb, tt, GR_W), blk),
                  pl.BlockSpec((nb, H_R, DK_R, DV_R), st),
                  _const_spec(dmat.shape), _const_spec(qdec.shape), _const_spec(kdec.shape),
                  _const_spec(sdec.shape)],
        out_specs=[pl.BlockSpec((nb, tt, VR_W), blk), pl.BlockSpec((nb, H_R, DK_R, DV_R), st)],
        out_shape=[jax.ShapeDtypeStruct((b, t, VR_W), out_dtype),
                   jax.ShapeDtypeStruct((b, H_R, DK_R, DV_R), F32)],
        scratch_shapes=[pltpu.VMEM((nb, H_R * DK_R, DV_R), F32)],
        compiler_params=_params("parallel", "arbitrary"), name="retention",
    )(qr, kr, vr, gr, s0, dmat, qdec, kdec, sdec)


def _mix_kernel(oa_ref, or_ref, h_ref, wo_ref, gpost_ref, gpre_ref, wq_ref, h1_ref, qm_ref, *, dh_m):
    mix_in = jnp.concatenate([oa_ref[...], or_ref[...]], axis=-1).astype(BF16)
    mix = jnp.dot(mix_in, wo_ref[...], preferred_element_type=F32)
    h1 = h_ref[...] + _rms(mix, gpost_ref[...])
    h1_ref[...] = h1
    xn = _rms(h1, gpre_ref[...]).astype(BF16)
    qm_ref[...] = (jnp.dot(xn, wq_ref[...], preferred_element_type=F32) * dh_m ** -0.5).astype(qm_ref.dtype)


def _mix(oa, orr, h, w_out, g_post, g_pre, w_mq, *, tm, act_dtype):
    n, d = h.shape
    row = lambda i: (i, 0)
    return pl.pallas_call(
        functools.partial(_mix_kernel, dh_m=d // H_M),
        grid=(n // tm,),
        in_specs=[pl.BlockSpec((tm, VA_W), row), pl.BlockSpec((tm, VR_W), row), pl.BlockSpec((tm, d), row),
                  _const_spec(w_out.shape), _const_spec((1, d)), _const_spec((1, d)), _const_spec(w_mq.shape)],
        out_specs=[pl.BlockSpec((tm, d), row), pl.BlockSpec((tm, d), row)],
        out_shape=[jax.ShapeDtypeStruct((n, d), F32), jax.ShapeDtypeStruct((n, d), act_dtype)],
        compiler_params=_params("parallel"), name="mix_out",
    )(oa, orr, h, w_out, g_post, g_pre, w_mq)


def _mem_chunk(h, c):
    return c * H_M + h


def _memkv_kernel(mem_ref, g_ref, wk_ref, wv_ref, mk_ref, mv_ref):
    mn = _rms(mem_ref[...], g_ref[...]).astype(BF16)
    tm, d = mn.shape
    halves = d // H_M // LANES
    nchunk = H_M * halves
    for w_ref, o_ref in ((wk_ref, mk_ref), (wv_ref, mv_ref)):
        y = jnp.dot(mn, w_ref[...], preferred_element_type=F32)
        for h in range(H_M):
            for c in range(halves):
                col = (h * halves + c) * LANES
                o_ref[pl.ds(_mem_chunk(h, c), tm, stride=nchunk), :] = y[:, col:col + LANES]


def _memkv(mem2d, g, w_mk, w_mv, *, tm):
    n, d = mem2d.shape
    nchunk = d // LANES
    row = lambda i: (i, 0)
    return pl.pallas_call(
        _memkv_kernel, grid=(n // tm,),
        in_specs=[pl.BlockSpec((tm, d), row), _const_spec((1, d)), _const_spec(w_mk.shape), _const_spec(w_mv.shape)],
        out_specs=[pl.BlockSpec((tm * nchunk, LANES), row)] * 2,
        out_shape=[jax.ShapeDtypeStruct((n * nchunk, LANES), F32)] * 2,
        compiler_params=_params("parallel"), name="mem_kv",
    )(mem2d, g, w_mk, w_mv)


def _memattn_kernel(q_ref, mk_ref, mv_ref, o_ref, *, mm, nb):
    d = q_ref.shape[-1]
    dh = d // H_M
    halves = dh // LANES
    nchunk = H_M * halves
    n_mem = mk_ref.shape[1] // nchunk

    def head(ref, bi, h):
        parts = [ref[bi, pl.ds(_mem_chunk(h, c), n_mem, stride=nchunk), :] for c in range(halves)]
        return jnp.concatenate(parts, axis=-1).astype(mm)

    for bi in range(nb):
        q = q_ref[bi]
        for h in range(H_M):
            qh = q[:, h * dh:(h + 1) * dh].astype(mm)
            s = lax.dot_general(qh, head(mk_ref, bi, h), _NT, preferred_element_type=F32)
            p = jnp.exp(s - jnp.max(s, axis=-1, keepdims=True))
            p = p * (1.0 / jnp.sum(p, axis=-1, keepdims=True))
            o = jnp.dot(p.astype(mm), head(mv_ref, bi, h), preferred_element_type=F32)
            o_ref[bi, :, h * dh:(h + 1) * dh] = o.astype(o_ref.dtype)


def _memattn(qm, mk, mv, *, tq, nb, mm):
    b, t, d = qm.shape
    rows = mk.shape[1]
    return pl.pallas_call(
        functools.partial(_memattn_kernel, mm=mm, nb=nb),
        grid=(b // nb, t // tq),
        in_specs=[pl.BlockSpec((nb, tq, d), lambda bi, qi: (bi, qi, 0)),
                  pl.BlockSpec((nb, rows, LANES), lambda bi, qi: (bi, 0, 0)),
                  pl.BlockSpec((nb, rows, LANES), lambda bi, qi: (bi, 0, 0))],
        out_specs=pl.BlockSpec((nb, tq, d), lambda bi, qi: (bi, qi, 0)),
        out_shape=jax.ShapeDtypeStruct((b, t, d), qm.dtype),
        compiler_params=_params("parallel", "arbitrary"), name="mem_attn",
    )(qm, mk, mv)


def _ffn_kernel(h1_ref, o_ref, wmo_ref, gpm_ref, gpre_ref, gpost_ref, wg_ref, wu_ref, cw_ref, cb_ref, wd_ref,
                *rest, tm, seq, fw, carry):
    if carry:
        prev_ref, y_ref, cnew_ref, a_scr, h_scr = rest
    else:
        p1_ref, p2_ref, y_ref, a_ref, a_scr, h_scr = rest
    i = pl.program_id(0)
    pad = SUBLANES
    m = jnp.dot(o_ref[...].astype(BF16), wmo_ref[...], preferred_element_type=F32)
    h2 = h1_ref[...] + _rms(m, gpm_ref[...])
    xn = _rms(h2, gpre_ref[...]).astype(BF16)
    d_ff = wg_ref.shape[1]

    if carry:
        @pl.when(i % (seq // tm) == 0)
        def _():
            a_scr[pad - 2:pad, :] = prev_ref[0]
    else:
        a_scr[0:pad, :] = jnp.zeros((pad, d_ff), F32)
        tpos = lax.broadcasted_iota(jnp.int32, (tm, fw), 0) % seq

    for c0 in range(0, d_ff, fw):
        cs = slice(c0, c0 + fw)
        a = jnp.dot(xn, wg_ref[:, cs], preferred_element_type=F32)
        a_scr[pad:pad + tm, cs] = a
        s1 = a_scr[pad - 1:pad - 1 + tm, cs]
        s2 = a_scr[pad - 2:pad - 2 + tm, cs]
        if not carry:
            s1 = jnp.where(tpos == 0, p1_ref[:, cs], s1)
            s2 = jnp.where(tpos < 2, p2_ref[:, cs], s2)
        conv = cb_ref[:, cs] + s2 * cw_ref[0:1, cs] + s1 * cw_ref[1:2, cs] + a * cw_ref[2:3, cs]
        up = jnp.dot(xn, wu_ref[:, cs], preferred_element_type=F32)
        h_scr[:, cs] = (_gelu_exact(conv) * up).astype(BF16)

    if carry:
        cnew_ref[0] = a_scr[pad + tm - 2:pad + tm, :]
        a_scr[0:pad, :] = a_scr[tm:tm + pad, :]
    else:
        a_ref[...] = a_scr[pad:pad + tm, :]
    f = jnp.dot(h_scr[...], wd_ref[...], preferred_element_type=F32)
    y_ref[...] = h2 + _rms(f, gpost_ref[...])


def _ffn(h1, o, w_mo, g_post_mem, g_pre, g_post, w_gate, w_up, conv_w, conv_b, w_down, prev, *, tm, seq, fw, carry):
    n, d = h1.shape
    d_ff = w_gate.shape[1]
    row = lambda i: (i, 0)
    in_specs = [pl.BlockSpec((tm, d), row), pl.BlockSpec((tm, d), row), _const_spec(w_mo.shape),
                _const_spec((1, d)), _const_spec((1, d)), _const_spec((1, d)),
                _const_spec(w_gate.shape), _const_spec(w_up.shape), _const_spec(conv_w.shape),
                _const_spec((1, d_ff)), _const_spec(w_down.shape)]
    out_specs = [pl.BlockSpec((tm, d), row)]
    out_shape = [jax.ShapeDtypeStruct((n, d), F32)]
    if carry:
        tps = seq // tm
        in_specs += [pl.BlockSpec((1, CONV_W - 1, d_ff), lambda i: (i // tps, 0, 0))]
        out_specs += [pl.BlockSpec((1, CONV_W - 1, d_ff), lambda i: (i // tps, 0, 0))]
        out_shape += [jax.ShapeDtypeStruct((n // seq, CONV_W - 1, d_ff), F32)]
        extra = (prev,)
    else:
        in_specs += [pl.BlockSpec((tm, d_ff), row)] * 2
        out_specs += [pl.BlockSpec((tm, d_ff), row)]
        out_shape += [jax.ShapeDtypeStruct((n, d_ff), F32)]
        extra = prev
    return pl.pallas_call(
        functools.partial(_ffn_kernel, tm=tm, seq=seq, fw=fw, carry=carry),
        grid=(n // tm,), in_specs=in_specs, out_specs=out_specs, out_shape=out_shape,
        scratch_shapes=[pltpu.VMEM((tm + SUBLANES, d_ff), F32), pltpu.VMEM((tm, d_ff), BF16)],
        compiler_params=_params("arbitrary"), name="mem_out_ffn",
    )(h1, o, w_mo, g_post_mem, g_pre, g_post, w_gate, w_up, conv_w, conv_b, w_down, *extra)


def _rotary_tables(pos, k_scale):
    half = DK_R // 2
    inv = 1.0 / (ROPE_BASE ** jnp.linspace(0.0, 1.0, half, dtype=F32))
    ang = pos.astype(F32)[:, None] * inv[None, :]
    cos, sin = jnp.cos(ang), jnp.sin(ang)
    cos_t = jnp.tile(jnp.concatenate([cos, cos], axis=-1), (1, H_R))
    sin_t = jnp.tile(jnp.concatenate([-sin, sin], axis=-1), (1, H_R))
    return cos_t, sin_t, cos_t * k_scale, sin_t * k_scale


def _retention_tables(c):
    log_g = jnp.log(1.0 - 2.0 ** (-5.0 - jnp.arange(H_R, dtype=F32)))
    i = jnp.arange(c, dtype=F32)
    diff = i[:, None] - i[None, :]
    dmat = jnp.where(diff[None] >= 0, jnp.exp(jnp.maximum(diff, 0.0)[None] * log_g[:, None, None]), 0.0)
    qdec = jnp.exp((i + 1.0)[:, None] * log_g[None, :])
    kdec = jnp.exp((c - 1.0 - i)[:, None] * log_g[None, :])
    sdec = jnp.exp(c * log_g)
    return (dmat, jnp.repeat(qdec, DV_R, axis=1), jnp.repeat(kdec, DK_R, axis=1),
            jnp.broadcast_to(jnp.repeat(sdec, DK_R)[:, None], (H_R * DK_R, DV_R)))


def _pick(n, pref):
    t = min(n, pref)
    while n % t:
        t //= 2
    return t


def kernel(x_prompt, x_sample, cache_k, cache_v, cache_mem_k, cache_mem_v, state_ret, state_conv, page_table, mem_prompt, w_in, w_out, lam_q1, lam_k1, lam_q2, lam_k2, g_diff_head, g_pre_mix, g_post_mix, g_pre_mem, g_post_mem, g_mem_in, w_mq, w_mk, w_mv, w_mo, g_pre_ffn, g_post_ffn, w_gate, w_up, conv_w, conv_b, w_down):
    b, s, d = x_prompt.shape
    bd, t, _ = x_sample.shape
    depth = w_in.shape[0]
    n_mem = mem_prompt.shape[1]
    d_ff = w_gate.shape[-1]
    past = page_table.shape[1] * PAGE_SIZE
    n_pool = cache_k.shape[1]
    assert depth == 1, "the sample path keeps the layer loop unrolled for a single layer"
    l = 0
    lam_init = 0.8 - 0.6 * math.exp(-0.3 * l)

    row = lambda v: v[l].reshape(1, -1)
    bf = lambda w: w[l].astype(BF16)
    w_in_b, w_out_b, w_mq_b, w_mk_b, w_mv_b, w_mo_b = map(bf, (w_in, w_out, w_mq, w_mk, w_mv, w_mo))
    w_gate_b, w_up_b, w_down_b = map(bf, (w_gate, w_up, w_down))
    lams = tuple(row(v) for v in (lam_q1, lam_k1, lam_q2, lam_k2))
    g_head = g_diff_head[l]

    tm_p = _pick(s, 512)
    tkv = _pick(s, 256)
    tq = _pick(s, 512)
    ret_c = RET_CHUNK if s % RET_CHUNK == 0 else s
    ret_c_s = RET_CHUNK if t % RET_CHUNK == 0 else t
    n_s = bd * t
    dh_m = d // H_M
    halves = dh_m // LANES
    nchunk = H_M * halves

    def mem_rows(a):
        g = a.shape[0]
        return a.reshape(g, n_mem, H_M, halves, LANES).transpose(0, 1, 3, 2, 4).reshape(g, n_mem * nchunk, LANES)

    def mem_heads(a):
        g = a.shape[0]
        return a.reshape(g, n_mem, halves, H_M, LANES).transpose(0, 1, 3, 2, 4).reshape(g, n_mem, H_M, dh_m)

    mk_p, mv_p = _memkv(mem_prompt.reshape(b * n_mem, d), row(g_mem_in), w_mk_b, w_mv_b, tm=_pick(b * n_mem, 512))
    mk_p = mk_p.reshape(b, n_mem * nchunk, LANES)
    mv_p = mv_p.reshape(b, n_mem * nchunk, LANES)
    tabs_p = _rotary_tables(jnp.arange(s), DK_R ** -0.5)
    qat, ka, va, kab, vat, qr, kr, vr, gr = _inproj(
        x_prompt.reshape(b * s, d), row(g_pre_mix), w_in_b, tabs_p, tm=tm_p, prompt=True, tkv=tkv, tqb=tq,
        q_scale=DH_A ** -0.5 * math.log2(math.e), act_dtype=BF16)
    tm_s = _pick(n_s, 512)
    reps = tm_s // t
    tabs_s = tuple(jnp.tile(a, (reps, 1)) for a in _rotary_tables(past + jnp.arange(t), DK_R ** -0.5))
    qa_s, ka_s, va_s, qr_s, kr_s, vr_s, gr_s = _inproj(
        x_sample.reshape(n_s, d), row(g_pre_mix), w_in_b, tabs_s, tm=tm_s, prompt=False, tkv=tkv, tqb=tq,
        q_scale=DH_A ** -0.5, act_dtype=F32)
    b3 = lambda a: a.reshape(bd, t, a.shape[-1])
    oa, oa_s = _dattn(page_table, lams, g_head, qat, kab, vat, b3(qa_s), b3(ka_s), b3(va_s),
                      cache_k[l].reshape(n_pool, PAGE_SIZE * H_A, 2 * DH_A),
                      cache_v[l].reshape(n_pool, PAGE_SIZE * H_A, DV_A),
                      b=b, s=s, tq=tq, tk=tkv, pp=_pick(page_table.shape[1], 16), lam_init=lam_init)
    r3 = lambda a: a.reshape(b, s, a.shape[-1])
    orr, ret_p = _retention(r3(qr), r3(kr), r3(vr), r3(gr), jnp.zeros((b, H_R, DK_R, DV_R), F32),
                            _retention_tables(ret_c), c=ret_c, cps=_pick(s // ret_c, 4), nb=1, mm=BF16, out_dtype=BF16)
    h1, qm = _mix(oa, orr.reshape(b * s, VR_W), x_prompt.reshape(b * s, d), w_out_b, row(g_post_mix),
                  row(g_pre_mem), w_mq_b, tm=tm_p, act_dtype=BF16)
    om = _memattn(qm.reshape(b, s, d), mk_p, mv_p, tq=tm_p, nb=1, mm=BF16)
    tm_f = _pick(s, 512)
    y_p, conv_p = _ffn(h1, om.reshape(b * s, d), w_mo_b, row(g_post_mem), row(g_pre_ffn), row(g_post_ffn),
                       w_gate_b, w_up_b, conv_w[l], row(conv_b), w_down_b,
                       jnp.zeros((b, CONV_W - 1, d_ff), F32), tm=tm_f, seq=s, fw=d_ff // 2, carry=True)

    orr_s, ret_s = _retention(b3(qr_s), b3(kr_s), b3(vr_s), b3(gr_s), state_ret[l], _retention_tables(ret_c_s),
                              c=ret_c_s, cps=_pick(t // ret_c_s, 4), nb=_pick(bd, 8), mm=F32, out_dtype=F32)
    h1_s, qm_s = _mix(oa_s.reshape(n_s, VA_W), orr_s.reshape(n_s, VR_W), x_sample.reshape(n_s, d), w_out_b,
                      row(g_post_mix), row(g_pre_mem), w_mq_b, tm=tm_s, act_dtype=F32)
    om_s = _memattn(b3(qm_s), mem_rows(cache_mem_k[l]), mem_rows(cache_mem_v[l]), tq=t, nb=_pick(bd, 4), mm=F32)
    prev = state_conv[l]
    zeros = jnp.zeros((bd, t - 2, d_ff), F32)
    p1 = jnp.concatenate([prev[:, 1:2], jnp.zeros((bd, 1, d_ff), F32), zeros], axis=1).reshape(n_s, d_ff)
    p2 = jnp.concatenate([prev, zeros], axis=1).reshape(n_s, d_ff)
    tm_fs = _pick(n_s, 256)
    y_s, a_s = _ffn(h1_s, om_s.reshape(n_s, d), w_mo_b, row(g_post_mem), row(g_pre_ffn), row(g_post_ffn),
                    w_gate_b, w_up_b, conv_w[l], row(conv_b), w_down_b, (p1, p2),
                    tm=tm_fs, seq=t, fw=d_ff // 2, carry=False)
    conv_s = a_s.reshape(bd, t, d_ff)[:, t - (CONV_W - 1):]

    return (y_p.reshape(b, s, d), y_s.reshape(bd, t, d),
            ka.reshape(1, b, s, H_A, 2 * DH_A), va.reshape(1, b, s, H_A, DV_A),
            mem_heads(mk_p)[None], mem_heads(mv_p)[None],
            ret_p[None], conv_p[None],
            ka_s.reshape(1, bd, t, H_A, 2 * DH_A), va_s.reshape(1, bd, t, H_A, DV_A),
            ret_s[None], conv_s[None])
```
